```python
import math
import jax, jax.numpy as jnp
from jax import lax
import numpy as np


D_MODEL = 2048
BATCH = 8
SEQ = 2048
DEPTH = 1

GRID_W = 64
CTX_LEN = 256
D_MIX = D_MODEL
D_LRU = D_MIX // 2
LRU_HEADS = 8
LRU_HEAD_DIM = D_LRU // LRU_HEADS
LRU_C = 8.0
D_GDN = D_MIX - D_LRU
GDN_HEADS = 8
GDN_HEAD_DIM = D_GDN // GDN_HEADS
N_DIR = 2
CONV_W = 4
CONV_PAD = (2, 1)
CHUNK = 64
D_FF = 4 * D_MODEL
NORM_EPS = 1e-6
LRU_X_END = D_LRU
LRU_Y_END = 2 * D_LRU
QKV_END = LRU_Y_END + 3 * D_GDN
Z_END = QKV_END + D_GDN
BETA_END = Z_END + N_DIR * GDN_HEADS
D_IN = BETA_END + N_DIR * GDN_HEADS

kernel_name = 'hymba_rglru_gdn_prefix_dit_block'


def rms_norm(t, w):
    tf = t.astype(jnp.float32)
    y = tf * lax.rsqrt(jnp.mean(tf * tf, axis=-1, keepdims=True) + NORM_EPS)
    return (y * w.astype(jnp.float32)).astype(t.dtype)


def l2_normalize(t):
    return t * lax.rsqrt(jnp.sum(t * t, axis=-1, keepdims=True) + NORM_EPS)


def to_column_major(t, rows):
    b, s, ch = t.shape
    return t.reshape(b, rows, GRID_W, ch).transpose(0, 2, 1, 3).reshape(b, s, ch)


def to_raster(t, rows):
    b, s, ch = t.shape
    return t.reshape(b, GRID_W, rows, ch).transpose(0, 2, 1, 3).reshape(b, s, ch)


def back_order(t, n_ctx):
    return jnp.concatenate([jnp.flip(t[:, :n_ctx], axis=1), jnp.flip(t[:, n_ctx:], axis=1)], axis=1)


def centred_depthwise_conv(t, w):
    return lax.conv_general_dilated(t, w[:, None, :].astype(t.dtype), window_strides=(1,),
                                    padding=[CONV_PAD], dimension_numbers=('NWC', 'WIO', 'NWC'),
                                    feature_group_count=t.shape[-1])


def _lru_combine(e1, e2):
    a1, u1 = e1
    a2, u2 = e2
    return a1 * a2, a2 * u1 + u2


def rglru_scan(seq, gate_w, gate_b, lam):
    b, n, _ = seq.shape
    xf = seq.astype(jnp.float32)
    gates = jnp.einsum('blhi,ghij->gblhj', xf.reshape(b, n, LRU_HEADS, LRU_HEAD_DIM),
                       gate_w.astype(jnp.float32)).reshape(2, b, n, D_LRU)
    r, i = jax.nn.sigmoid(gates + gate_b.astype(jnp.float32)[:, None, None, :])
    log_a = -LRU_C * r * jax.nn.softplus(-lam.astype(jnp.float32))
    a = jnp.exp(log_a)
    u = jnp.sqrt(-jnp.expm1(2.0 * log_a)) * (i * xf)
    _, h = lax.associative_scan(_lru_combine, (a, u), axis=1)
    return h


def gated_delta_chunked(q, k, v, g, beta):
    b, n_tok, h, dk = q.shape
    dv = v.shape[-1]
    nc = n_tok // CHUNK
    f32 = jnp.float32

    def chunks(t):
        return t.astype(f32).reshape(b, nc, CHUNK, h, -1).transpose(0, 3, 1, 2, 4)

    q_c, k_c, v_c = chunks(q), chunks(k), chunks(v)
    g_c = g.astype(f32).reshape(b, nc, CHUNK, h).transpose(0, 3, 1, 2)
    b_c = beta.astype(f32).reshape(b, nc, CHUNK, h).transpose(0, 3, 1, 2)
    g_cum = jnp.cumsum(g_c, axis=-1)
    tril = jnp.tril(jnp.ones((CHUNK, CHUNK), dtype=bool))
    decay = jnp.exp(jnp.where(tril, g_cum[..., :, None] - g_cum[..., None, :], -jnp.inf))
    k_beta = k_c * b_c[..., None]
    v_beta = v_c * b_c[..., None]
    l_mat = jnp.einsum('bhncd,bhnsd->bhncs', k_beta, k_c) * decay
    rhs = jnp.concatenate([v_beta, k_beta * jnp.exp(g_cum)[..., None]], axis=-1)
    sol = lax.linalg.triangular_solve(l_mat, rhs, left_side=True, lower=True, unit_diagonal=True)
    u_c, w_c = sol[..., :dv], sol[..., dv:]
    qk = jnp.einsum('bhncd,bhnsd->bhncs', q_c, k_c) * decay
    q_dec = q_c * jnp.exp(g_cum)[..., None]
    k_dec = k_c * jnp.exp(g_cum[..., -1:] - g_cum)[..., None]
    chunk_decay = jnp.exp(g_cum[..., -1])

    def step(state, xs):
        qk_i, q_dec_i, k_dec_i, u_i, w_i, cd_i = xs
        v_new = u_i - jnp.einsum('bhck,bhkv->bhcv', w_i, state)
        o_i = jnp.einsum('bhck,bhkv->bhcv', q_dec_i, state) + jnp.einsum('bhcs,bhsv->bhcv', qk_i, v_new)
        state = state * cd_i[..., None, None] + jnp.einsum('bhck,bhcv->bhkv', k_dec_i, v_new)
        return state, o_i

    xs = tuple(jnp.moveaxis(t, 2, 0) for t in (qk, q_dec, k_dec, u_c, w_c, chunk_decay))
    state0 = jnp.zeros((b, h, dk, dv), f32)
    _, o = lax.scan(step, state0, xs)
    return o.transpose(1, 0, 3, 2, 4).reshape(b, n_tok, h, dv)


def hybrid_mixer(h_ctx, h_lat, rows, need_ctx, w_in, lru_conv_w, lru_conv_b, lru_gate_w, lru_gate_b,
                 lru_lambda, gdn_conv_w, gdn_a_log, gdn_dt_bias, gdn_norm_w, w_out):
    n_ctx = h_ctx.shape[1]
    p = jnp.concatenate([h_ctx, h_lat], axis=1) @ w_in
    b, n_all, _ = p.shape
    p_ctx, p_lat = p[:, :n_ctx], p[:, n_ctx:]

    xr = jnp.concatenate([centred_depthwise_conv(p_ctx[..., :LRU_X_END], lru_conv_w),
                          centred_depthwise_conv(to_column_major(p_lat[..., :LRU_X_END], rows), lru_conv_w)],
                         axis=1) + lru_conv_b
    h_lru = rglru_scan(xr, lru_gate_w[0], lru_gate_b[0], lru_lambda[0]) + back_order(
        rglru_scan(back_order(xr, n_ctx), lru_gate_w[1], lru_gate_b[1], lru_lambda[1]), n_ctx)
    h_lru = jnp.concatenate([h_lru[:, :n_ctx], to_raster(h_lru[:, n_ctx:], rows)], axis=1).astype(p.dtype)
    y_gate = jax.nn.gelu(p[..., LRU_X_END:LRU_Y_END])

    qkv = jnp.concatenate([centred_depthwise_conv(p_ctx[..., LRU_Y_END:QKV_END], gdn_conv_w),
                           centred_depthwise_conv(p_lat[..., LRU_Y_END:QKV_END], gdn_conv_w)], axis=1)
    qkv = jax.nn.silu(qkv).astype(jnp.float32).reshape(b, n_all, 3, GDN_HEADS, GDN_HEAD_DIM)
    q = l2_normalize(qkv[:, :, 0]) * GDN_HEAD_DIM ** -0.5
    k = l2_normalize(qkv[:, :, 1])
    v = qkv[:, :, 2]
    gb = p[..., Z_END:].astype(jnp.float32).reshape(b, n_all, 2, N_DIR, GDN_HEADS)
    beta = jax.nn.sigmoid(gb[:, :, 0])
    g = -jnp.exp(gdn_a_log.astype(jnp.float32)) * jax.nn.softplus(gb[:, :, 1] + gdn_dt_bias.astype(jnp.float32))
    o = gated_delta_chunked(q, k, v, g[:, :, 0], beta[:, :, 0]) + back_order(
        gated_delta_chunked(back_order(q, n_ctx), back_order(k, n_ctx), back_order(v, n_ctx),
                            back_order(g[:, :, 1], n_ctx), back_order(beta[:, :, 1], n_ctx)), n_ctx)
    z = p[..., QKV_END:Z_END]

    def project(lo, hi):
        gdn = rms_norm(o[:, lo:hi], gdn_norm_w).reshape(b, hi - lo, D_GDN).astype(p.dtype) * jax.nn.silu(z[:, lo:hi])
        lru = h_lru[:, lo:hi] * y_gate[:, lo:hi]
        return jnp.concatenate([lru, gdn], axis=-1) @ w_out

    out_lat = project(n_ctx, n_all)
    out_ctx = project(0, n_ctx) if need_ctx else None
    return out_ctx, out_lat


def sq_relu_mlp(h, w1, w2):
    return jnp.square(jax.nn.relu(h @ w1)) @ w2


def setup_inputs(seed: int = 0) -> dict:
    key = jax.random.key(seed)
    ks = jax.random.split(key, 20)
    f32 = jnp.float32

    def nrm(k, shape, scale):
        return jax.random.normal(k, shape, f32) * scale

    x = nrm(ks[0], (BATCH, SEQ, D_MODEL), 1.0)
    c = nrm(ks[1], (BATCH, D_MODEL), 1.0)
    ctx = nrm(ks[2], (BATCH, CTX_LEN, D_MODEL), 1.0)
    c_ctx = nrm(ks[3], (D_MODEL,), 1.0)
    w_mod = nrm(ks[4], (DEPTH, D_MODEL, 6 * D_MODEL), 0.5 * D_MODEL ** -0.5)
    b_mod = nrm(ks[5], (DEPTH, 6 * D_MODEL), 0.01)
    norm_w = 1.0 + nrm(ks[6], (DEPTH, 4, D_MODEL), 0.01)
    w_in = nrm(ks[7], (DEPTH, D_MODEL, D_IN), D_MODEL ** -0.5)
    lru_conv_w = nrm(ks[8], (DEPTH, CONV_W, D_LRU), CONV_W ** -0.5)
    lru_conv_b = nrm(ks[9], (DEPTH, D_LRU), 0.01)
    lru_gate_w = nrm(ks[10], (DEPTH, N_DIR, 2, LRU_HEADS, LRU_HEAD_DIM, LRU_HEAD_DIM), LRU_HEAD_DIM ** -0.5)
    lru_gate_b = nrm(ks[11], (DEPTH, N_DIR, 2, D_LRU), 0.01)
    a_pow = jax.random.uniform(ks[12], (DEPTH, N_DIR, D_LRU), f32, 0.9, 0.999)
    a0 = a_pow ** (1.0 / LRU_C)
    lru_lambda = jnp.log(a0) - jnp.log1p(-a0)
    gdn_conv_w = nrm(ks[13], (DEPTH, CONV_W, 3 * D_GDN), CONV_W ** -0.5)
    gdn_a_log = jnp.log(jax.random.uniform(ks[14], (DEPTH, N_DIR, GDN_HEADS), f32, 1.0, 16.0))
    dt = jnp.exp(jax.random.uniform(ks[15], (DEPTH, N_DIR, GDN_HEADS), f32, math.log(0.001), math.log(0.1)))
    gdn_dt_bias = dt + jnp.log(-jnp.expm1(-dt))
    gdn_norm_w = 1.0 + nrm(ks[16], (DEPTH, GDN_HEAD_DIM), 0.01)
    w_out = nrm(ks[17], (DEPTH, D_MIX, D_MODEL), D_MIX ** -0.5)
    w_ff1 = nrm(ks[18], (DEPTH, D_MODEL, D_FF), D_MODEL ** -0.5)
    w_ff2 = nrm(ks[19], (DEPTH, D_FF, D_MODEL), D_FF ** -0.5)
    return {'x': x, 'c': c, 'ctx': ctx, 'c_ctx': c_ctx, 'w_mod': w_mod, 'b_mod': b_mod, 'norm_w': norm_w,
            'w_in': w_in, 'lru_conv_w': lru_conv_w, 'lru_conv_b': lru_conv_b, 'lru_gate_w': lru_gate_w,
            'lru_gate_b': lru_gate_b, 'lru_lambda': lru_lambda, 'gdn_conv_w': gdn_conv_w,
            'gdn_a_log': gdn_a_log, 'gdn_dt_bias': gdn_dt_bias, 'gdn_norm_w': gdn_norm_w,
            'w_out': w_out, 'w_ff1': w_ff1, 'w_ff2': w_ff2}


def reference(x, c, ctx, c_ctx, w_mod, b_mod, norm_w, w_in, lru_conv_w, lru_conv_b, lru_gate_w,
              lru_gate_b, lru_lambda, gdn_conv_w, gdn_a_log, gdn_dt_bias, gdn_norm_w, w_out, w_ff1, w_ff2):
    rows = x.shape[1] // GRID_W
    for layer in range(DEPTH):
        need_ctx = layer < DEPTH - 1
        mod = jax.nn.silu(c) @ w_mod[layer] + b_mod[layer]
        mod_c = jax.nn.silu(c_ctx) @ w_mod[layer] + b_mod[layer]
        sh_m, sc_m, g_m, sh_f, sc_f, g_f = jnp.split(mod[:, None, :], 6, axis=-1)
        csh_m, csc_m, cg_m, csh_f, csc_f, cg_f = jnp.split(mod_c, 6, axis=-1)
        nw = norm_w[layer]

        h_lat = rms_norm(x, nw[0]) * (1.0 + sc_m) + sh_m
        h_ctx = rms_norm(ctx, nw[0]) * (1.0 + csc_m) + csh_m
        m_ctx, m_lat = hybrid_mixer(h_ctx, h_lat, rows, need_ctx, w_in[layer], lru_conv_w[layer],
                                    lru_conv_b[layer], lru_gate_w[layer], lru_gate_b[layer], lru_lambda[layer],
                                    gdn_conv_w[layer], gdn_a_log[layer], gdn_dt_bias[layer], gdn_norm_w[layer],
                                    w_out[layer])
        x = x + g_m * rms_norm(m_lat, nw[1])
        h = rms_norm(x, nw[2]) * (1.0 + sc_f) + sh_f
        x = x + g_f * rms_norm(sq_relu_mlp(h, w_ff1[layer], w_ff2[layer]), nw[3])

        if need_ctx:
            ctx = ctx + cg_m * rms_norm(m_ctx, nw[1])
            hc = rms_norm(ctx, nw[2]) * (1.0 + csc_f) + csh_f
            ctx = ctx + cg_f * rms_norm(sq_relu_mlp(hc, w_ff1[layer], w_ff2[layer]), nw[3])
    return x
```

```python
import functools
import math

import jax
import jax.numpy as jnp
from jax import lax
from jax.experimental import pallas as pl
from jax.experimental.pallas import tpu as pltpu

F32 = jnp.float32
BF16 = jnp.bfloat16

LANES = 128
SUBLANES = 8
VMEM_LIMIT = 56 * 1024 * 1024

GRID_W = 64
NORM_EPS = 1e-6
LRU_C = 8.0
LRU_HEADS = 8
GDN_HEADS = 8
N_DIR = 2
CHUNK = 64
NEG_BIG = -1e30


def _cparams(sem):
    return pltpu.CompilerParams(dimension_semantics=sem, vmem_limit_bytes=VMEM_LIMIT)


def _dot(a, b):
    return jnp.dot(a, b, preferred_element_type=F32)


def _dot_nt(a, b):
    return lax.dot_general(a, b, (((1,), (1,)), ((), ())), preferred_element_type=F32)


def _sigmoid(t):
    return 1.0 / (1.0 + jnp.exp(-t))


def _softplus(t):
    return jnp.maximum(t, 0.0) + jnp.log(1.0 + jnp.exp(-jnp.abs(t)))


def _silu(t):
    return t * _sigmoid(t)


def _gelu_tanh(t):
    return 0.5 * t * (1.0 + jnp.tanh(math.sqrt(2.0 / math.pi) * (t + 0.044715 * (t * t * t))))


def _rms_rows(t):
    return t * lax.rsqrt(jnp.mean(t * t, axis=-1, keepdims=True) + NORM_EPS)


def _mod_kernel(c_ref, w_ref, b_ref, o_ref):
    s = _silu(c_ref[...])
    o_ref[...] = _dot(s.astype(BF16), w_ref[...].astype(BF16)) + b_ref[...]


def _modulation(cc, w_mod, b_mod):
    rows, d = cc.shape
    n = w_mod.shape[1]
    tn = 1024
    return pl.pallas_call(
        _mod_kernel,
        grid=(n // tn,),
        in_specs=[pl.BlockSpec((rows, d), lambda j: (0, 0)),
                  pl.BlockSpec((d, tn), lambda j: (0, j)),
                  pl.BlockSpec((1, tn), lambda j: (0, j))],
        out_specs=pl.BlockSpec((rows, tn), lambda j: (0, j)),
        out_shape=jax.ShapeDtypeStruct((rows, n), F32),
        compiler_params=_cparams(("arbitrary",)),
        name="mod",
    )(cc, w_mod, b_mod.reshape(1, n))


def _inproj_raster_kernel(x_ref, nw_ref, sc_ref, sh_ref, w_ref, wgb_ref, p_ref, gb_ref, h_scr, *, tn):
    @pl.when(pl.program_id(2) == 0)
    def _():
        a = nw_ref[...] * (1.0 + sc_ref[0])
        h = (_rms_rows(x_ref[0]) * a + sh_ref[0]).astype(BF16)
        h_scr[...] = h
        gb_ref[0] = _dot(h, wgb_ref[...])

    acc = _dot(h_scr[...], w_ref[...])
    for j in range(tn // LANES):
        p_ref[0, j] = acc[:, j * LANES:(j + 1) * LANES].astype(p_ref.dtype)


def _inproj_raster(x, nw, sc, sh, w, wgb, tm, tn):
    b, l, d = x.shape
    n = w.shape[1]
    return pl.pallas_call(
        functools.partial(_inproj_raster_kernel, tn=tn),
        grid=(b, l // tm, n // tn),
        in_specs=[pl.BlockSpec((1, tm, d), lambda i, m, j: (i, m, 0)),
                  pl.BlockSpec((1, d), lambda i, m, j: (0, 0)),
                  pl.BlockSpec((1, 1, d), lambda i, m, j: (i, 0, 0)),
                  pl.BlockSpec((1, 1, d), lambda i, m, j: (i, 0, 0)),
                  pl.BlockSpec((d, tn), lambda i, m, j: (0, j)),
                  pl.BlockSpec((d, LANES), lambda i, m, j: (0, 0))],
        out_specs=[pl.BlockSpec((1, tn // LANES, tm, LANES), lambda i, m, j: (i, j, m, 0)),
                   pl.BlockSpec((1, tm, LANES), lambda i, m, j: (i, m, 0))],
        out_shape=[jax.ShapeDtypeStruct((b, n // LANES, l, LANES), BF16),
                   jax.ShapeDtypeStruct((b, l, LANES), F32)],
        scratch_shapes=[pltpu.VMEM((tm, d), BF16)],
        compiler_params=_cparams(("parallel", "parallel", "arbitrary")),
        name="inproj_raster",
    )(x, nw, sc, sh, w, wgb)


def _inproj_tb_kernel(x_ref, nw_ref, sc_ref, sh_ref, w_ref, o_ref, h_scr, *, r, wb, d, nb):
    per_b = r * wb
    for i in range(nb):
        a = nw_ref[...] * (1.0 + sc_ref[i:i + 1, :])
        s = sh_ref[i:i + 1, :]
        for w in range(wb):
            h = _rms_rows(x_ref[i, :, w * d:(w + 1) * d]) * a + s
            h_scr[(i * wb + w) * r:(i * wb + w + 1) * r, :] = h.astype(BF16)
    acc = _dot(h_scr[...], w_ref[...])
    for j in range(w_ref.shape[1] // LANES):
        for i in range(nb):
            o_ref[j, pl.ds(i, per_b, stride=nb), :] = acc[i * per_b:(i + 1) * per_b, j * LANES:(j + 1) * LANES]


def _inproj_tb(xv, nw, sc, sh, w, r, wb):
    nb, r_total, wd = xv.shape
    d = w.shape[0]
    n = w.shape[1]
    assert r_total % r == 0 and (wd // d) % wb == 0
    if r_total != r:
        assert wb == 1 and wd == d
        grid = (r_total // r,)
        x_spec = pl.BlockSpec((nb, r, d), lambda m: (0, m, 0))
    else:
        grid = (wd // (wb * d),)
        x_spec = pl.BlockSpec((nb, r, wb * d), lambda m: (0, 0, m))
    rows_blk = r * wb * nb
    total_rows = r_total * (wd // d) * nb
    return pl.pallas_call(
        functools.partial(_inproj_tb_kernel, r=r, wb=wb, d=d, nb=nb),
        grid=grid,
        in_specs=[x_spec,
                  pl.BlockSpec((1, d), lambda m: (0, 0)),
                  pl.BlockSpec((nb, d), lambda m: (0, 0)),
                  pl.BlockSpec((nb, d), lambda m: (0, 0)),
                  pl.BlockSpec((d, n), lambda m: (0, 0))],
        out_specs=pl.BlockSpec((n // LANES, rows_blk, LANES), lambda m: (0, m, 0)),
        out_shape=jax.ShapeDtypeStruct((n // LANES, total_rows, LANES), F32),
        scratch_shapes=[pltpu.VMEM((rows_blk, d), BF16)],
        compiler_params=_cparams(("parallel",)),
        name="inproj_tb",
    )(xv, nw, sc, sh, w)


def _lru_kernel(xc_ref, xl_ref, cw_ref, cb_ref, wg_ref, gbias_ref, lam_ref, o_ref,
                hsum, af, uf, ab, ub, *, lc, s, rows, tseg, nb):
    hd = LANES
    pitch = rows * nb + SUBLANES
    cols_seg = tseg // rows
    cw = cw_ref[...]
    cb = cb_ref[...]
    sp = _softplus(-lam_ref[...])

    def seg_xr(ref, t0, lseq):
        main = ref[pl.ds(pl.multiple_of(t0 * nb, SUBLANES), tseg * nb), :]
        p0 = jnp.maximum(t0 * nb - 2 * nb, 0)
        prev = ref[pl.ds(pl.multiple_of(p0, SUBLANES), 2 * nb), :]
        prev = jnp.where(t0 > 0, prev, 0.0)
        n0 = jnp.minimum((t0 + tseg) * nb, lseq * nb - nb)
        nxt = ref[pl.ds(pl.multiple_of(n0, SUBLANES), nb), :]
        nxt = jnp.where(t0 + tseg < lseq, nxt, 0.0)
        xm2 = jnp.concatenate([prev, main[:-2 * nb]], axis=0)
        xm1 = jnp.concatenate([prev[nb:], main[:-nb]], axis=0)
        xp1 = jnp.concatenate([main[nb:], nxt], axis=0)
        return cw[0:1] * xm2 + cw[1:2] * xm1 + cw[2:3] * main + cw[3:4] * xp1 + cb

    def gates(xr, direction, a_ref, u_ref):
        lo = direction * 2 * hd
        g = _dot(xr.astype(BF16), wg_ref[0, :, lo:lo + 2 * hd]) + gbias_ref[0, :, lo:lo + 2 * hd]
        rg = _sigmoid(g[:, :hd])
        ig = _sigmoid(g[:, hd:])
        log_a = (-LRU_C) * rg * sp[:, direction * hd:(direction + 1) * hd]
        a_ref[...] = jnp.exp(log_a)
        th = jnp.tanh(log_a)
        u_ref[...] = jnp.sqrt(-2.0 * th / (1.0 - th)) * (ig * xr)

    def scan_segment(hf, hb, store, fseg, bseg):
        def body(i, carry):
            hf, hb = carry
            rf = pl.multiple_of(i * nb, SUBLANES)
            rb = pl.multiple_of((tseg - 1 - i) * nb, SUBLANES)
            hf = af[pl.ds(rf, nb), :] * hf + uf[pl.ds(rf, nb), :]
            hb = ab[pl.ds(rb, nb), :] * hb + ub[pl.ds(rb, nb), :]
            if store:
                jf = fseg * tseg + i
                jb = bseg * tseg + (tseg - 1 - i)
                pf = pl.multiple_of((jf // rows) * pitch + (jf % rows) * nb, SUBLANES)
                pb = pl.multiple_of((jb // rows) * pitch + (jb % rows) * nb, SUBLANES)
                hsum[pl.ds(pf, nb), :] = hsum[pl.ds(pf, nb), :] + hf
                hsum[pl.ds(pb, nb), :] = hsum[pl.ds(pb, nb), :] + hb
            return hf, hb
        return lax.fori_loop(0, tseg, body, (hf, hb))

    hsum[...] = jnp.zeros_like(hsum)
    h0 = jnp.zeros((nb, hd), F32)

    nseg_c = lc // tseg
    def ctx_body(i, carry):
        gates(seg_xr(xc_ref.at[0], i * tseg, lc), 0, af, uf)
        gates(seg_xr(xc_ref.at[0], (nseg_c - 1 - i) * tseg, lc), 1, ab, ub)
        return scan_segment(carry[0], carry[1], False, 0, 0)
    hf, hb = lax.fori_loop(0, nseg_c, ctx_body, (h0, h0))

    nseg_l = s // tseg
    def lat_body(i, carry):
        bseg = nseg_l - 1 - i
        gates(seg_xr(xl_ref.at[0], i * tseg, s), 0, af, uf)
        gates(seg_xr(xl_ref.at[0], bseg * tseg, s), 1, ab, ub)
        return scan_segment(carry[0], carry[1], True, i, bseg)
    lax.fori_loop(0, nseg_l, lat_body, (hf, hb))

    del cols_seg
    def out_body(rr, carry):
        for i in range(nb):
            for wg in range(GRID_W // SUBLANES):
                src = wg * SUBLANES * pitch + rr * nb + i
                o_ref[i, pl.ds(pl.multiple_of(rr * GRID_W + wg * SUBLANES, SUBLANES), SUBLANES), :] = (
                    hsum[pl.ds(src, SUBLANES, stride=pitch), :])
        return carry
    lax.fori_loop(0, rows, out_body, 0)


def _lru(xc, xl, conv_w, conv_b, wg, gbias, lam2, nb, lc, s):
    heads = xl.shape[0]
    rows = s // GRID_W
    tseg = 128
    assert lc % tseg == 0 and s % tseg == 0 and tseg % rows == 0
    pitch = rows * nb + SUBLANES
    return pl.pallas_call(
        functools.partial(_lru_kernel, lc=lc, s=s, rows=rows, tseg=tseg, nb=nb),
        grid=(heads,),
        in_specs=[pl.BlockSpec((1, lc * nb, LANES), lambda h: (h, 0, 0)),
                  pl.BlockSpec((1, s * nb, LANES), lambda h: (h, 0, 0)),
                  pl.BlockSpec((4, LANES), lambda h: (0, h)),
                  pl.BlockSpec((1, LANES), lambda h: (0, h)),
                  pl.BlockSpec((1, LANES, 4 * LANES), lambda h: (h, 0, 0)),
                  pl.BlockSpec((1, 1, 4 * LANES), lambda h: (h, 0, 0)),
                  pl.BlockSpec((1, 2 * LANES), lambda h: (0, h))],
        out_specs=pl.BlockSpec((nb, s, LANES), lambda h: (0, 0, h)),
        out_shape=jax.ShapeDtypeStruct((nb, s, heads * LANES), F32),
        scratch_shapes=[pltpu.VMEM((GRID_W * pitch, LANES), F32)]
                       + [pltpu.VMEM((tseg * nb, LANES), F32)] * 4,
        compiler_params=_cparams(("parallel",)),
        name="lru",
    )(xc, xl, conv_w, conv_b, wg, gbias, lam2)


def _inv_unit_triangular(lm, eye):
    c = lm.shape[0]
    p = eye - lm
    lb = lm.astype(BF16)
    lk = _dot(lb, lb)
    n_sq = int(math.log2(c)) - 2
    for _ in range(n_sq):
        lkb = lk.astype(BF16)
        x = _dot(jnp.concatenate([p.astype(BF16), lkb], axis=0), lkb)
        p = p + x[:c]
        lk = x[c:]
    return p + _dot(p.astype(BF16), lk.astype(BF16))


def _gdn_kernel(qc_ref, kc_ref, vc_ref, ql_ref, kl_ref, vl_ref, z_ref, g_ref, alog_ref, dtb_ref,
                cwq_ref, cwk_ref, cwv_ref, nw_ref, o_ref,
                qs, ks, vs, qcol, grow, u_s, w_s, qd_s, kdt_s, qkd_s, st_s, o_s, *, lc, s):
    dk = LANES
    c = CHUNK
    l_all = lc + s
    ncc = lc // c
    ncl = s // c
    nct = ncc + ncl

    def conv_silu(ref, cw, n):
        x = ref[0, 0].astype(F32)
        rid = lax.broadcasted_iota(jnp.int32, (n, dk), 0)
        xm2 = jnp.where(rid >= 2, pltpu.roll(x, 2, 0), 0.0)
        xm1 = jnp.where(rid >= 1, pltpu.roll(x, 1, 0), 0.0)
        xp1 = jnp.where(rid < n - 1, pltpu.roll(x, n - 1, 0), 0.0)
        return _silu(cw[0:1] * xm2 + cw[1:2] * xm1 + cw[2:3] * x + cw[3:4] * xp1)

    def l2n(t):
        return t * lax.rsqrt(jnp.sum(t * t, axis=-1, keepdims=True) + NORM_EPS)

    cwq = cwq_ref[...]
    cwk = cwk_ref[...]
    cwv = cwv_ref[...]
    qs[0:lc] = l2n(conv_silu(qc_ref, cwq, lc)) * (dk ** -0.5)
    qs[lc:l_all] = l2n(conv_silu(ql_ref, cwq, s)) * (dk ** -0.5)
    ks[0:lc] = l2n(conv_silu(kc_ref, cwk, lc))
    ks[lc:l_all] = l2n(conv_silu(kl_ref, cwk, s))
    vs[0:lc] = conv_silu(vc_ref, cwv, lc)
    vs[lc:l_all] = conv_silu(vl_ref, cwv, s)

    r = g_ref[0, 0]
    rid8 = lax.broadcasted_iota(jnp.int32, (SUBLANES, l_all), 0)
    aneg = -jnp.exp(alog_ref[0][:, 0:1])
    dtb = dtb_ref[0][:, 0:1]
    val = jnp.where(rid8 < 2, _sigmoid(r), aneg * _softplus(r + dtb))
    blk = 2 * LANES
    si = lax.broadcasted_iota(jnp.int32, (blk, blk), 0)
    ji = lax.broadcasted_iota(jnp.int32, (blk, blk), 1)
    same = (si // c) == (ji // c)
    t_pre = jnp.where(same & (si <= ji), 1.0, 0.0).astype(F32)
    t_suf = jnp.where(same & (si >= ji), 1.0, 0.0).astype(F32)
    for i in range(l_all // blk):
        vb = val[:, i * blk:(i + 1) * blk]
        pre = jnp.dot(vb, t_pre, preferred_element_type=F32, precision=lax.Precision.HIGHEST)
        suf = jnp.dot(vb, t_suf, preferred_element_type=F32, precision=lax.Precision.HIGHEST)
        rb = lax.broadcasted_iota(jnp.int32, (SUBLANES, blk), 0)
        grow[:, i * blk:(i + 1) * blk] = jnp.where(rb == 2, pre, jnp.where(rb == 3, suf, vb))
    gfull = jnp.concatenate([grow[...], jnp.zeros((LANES - SUBLANES, l_all), F32)], axis=0)
    qcol[...] = gfull.T

    ii = lax.broadcasted_iota(jnp.int32, (c, c), 0)
    jj = lax.broadcasted_iota(jnp.int32, (c, c), 1)
    eye = jnp.where(ii == jj, 1.0, 0.0).astype(F32)

    def intra(pair, carry):
        r0 = pl.multiple_of(pair * 2 * c, 2 * c)
        grow_pair = grow[:, pl.ds(r0, 2 * c)]
        for sub in range(2):
            rs = pl.multiple_of(r0 + sub * c, c)
            ci = pair * 2 + sub
            q = qs[pl.ds(rs, c), :]
            k = ks[pl.ds(rs, c), :]
            v = vs[pl.ds(rs, c), :]
            col = qcol[pl.ds(rs, c), :]
            rowf = grow_pair[:, sub * c:(sub + 1) * c]
            kb16 = k.astype(BF16)
            both = _dot_nt(jnp.concatenate([q.astype(BF16), kb16], axis=0), kb16)
            qk = both[:c]
            kk = both[c:]
            qkd = []
            for d in range(N_DIR):
                beta = col[:, d:d + 1]
                gcol = col[:, 2 + d:3 + d]
                grw = rowf[2 + d:3 + d, :]
                keep = (ii >= jj) if d == 0 else (ii <= jj)
                strict = (ii > jj) if d == 0 else (ii < jj)
                decay = jnp.exp(jnp.where(keep, gcol - grw, NEG_BIG))
                lm = jnp.where(strict, beta * kk * decay, 0.0)
                tinv = _inv_unit_triangular(lm, eye)
                eg = jnp.exp(gcol)
                kbeta = k * beta
                rhs = jnp.concatenate([v * beta, kbeta * eg], axis=1).astype(BF16)
                uw = _dot(tinv.astype(BF16), rhs)
                glast = gcol[c - 1:c, :] if d == 0 else gcol[0:1, :]
                u_s[d, pl.ds(rs, c), :] = uw[:, :dk]
                w_s[d, pl.ds(rs, c), :] = uw[:, dk:].astype(BF16)
                qd_s[d, pl.ds(rs, c), :] = (q * eg).astype(BF16)
                kdt_s[d, ci] = (k * jnp.exp(glast - gcol)).T.astype(BF16)
                qkd.append(qk * decay)
            qkd_s[pl.ds(rs, c), :] = jnp.concatenate(qkd, axis=1).astype(BF16)
        return carry
    lax.fori_loop(0, nct // 2, intra, 0)

    st_s[...] = jnp.zeros_like(st_s)
    o_s[...] = jnp.zeros_like(o_s)

    def recur(t, carry):
        cf = t
        cb = jnp.where(t < ncc, ncc - 1 - t, 2 * ncc + ncl - 1 - t)
        for d, ci in ((0, cf), (1, cb)):
            rs = pl.multiple_of(ci * c, c)
            st = st_s[d]
            stb = st.astype(BF16)
            ws_qs = _dot(jnp.concatenate([w_s[d, pl.ds(rs, c), :], qd_s[d, pl.ds(rs, c), :]], axis=0), stb)
            v_new = u_s[d, pl.ds(rs, c), :] - ws_qs[:c]
            vnb = v_new.astype(BF16)
            qkd = qkd_s[pl.ds(rs, c), :][:, d * c:(d + 1) * c]
            o = ws_qs[c:] + _dot(qkd, vnb)
            edge = rs + (c - 1 if d == 0 else 0)
            cd = jnp.exp(qcol[pl.ds(edge, 1), :][:, 2 + d:3 + d])
            st_s[d] = st * cd + _dot(kdt_s[d, ci], vnb)

            @pl.when(t >= ncc)
            def _():
                ro = pl.multiple_of(rs - lc, c)
                o_s[pl.ds(ro, c), :] = o_s[pl.ds(ro, c), :] + o
        return carry
    lax.fori_loop(0, nct, recur, 0)

    z = z_ref[0, 0].astype(F32)
    o_ref[0] = ((_rms_rows(o_s[...]) * nw_ref[...]).astype(F32) * _silu(z)).astype(o_ref.dtype)


def _gdn(pc, plat, grow, alog8, dtb8, conv_w, norm_w, lc, s):
    b = plat.shape[0]
    heads = GDN_HEADS
    l_all = lc + s
    nct = l_all // CHUNK
    assert lc % CHUNK == 0 and s % CHUNK == 0 and nct % 2 == 0 and l_all % (2 * LANES) == 0
    def slot(base):
        return lambda i, h: (i, base + h, 0, 0)
    def cw(base):
        return lambda i, h: (0, base + h)
    return pl.pallas_call(
        functools.partial(_gdn_kernel, lc=lc, s=s),
        grid=(b, heads),
        in_specs=[pl.BlockSpec((1, 1, lc, LANES), slot(0)),
                  pl.BlockSpec((1, 1, lc, LANES), slot(heads)),
                  pl.BlockSpec((1, 1, lc, LANES), slot(2 * heads)),
                  pl.BlockSpec((1, 1, s, LANES), slot(heads)),
                  pl.BlockSpec((1, 1, s, LANES), slot(2 * heads)),
                  pl.BlockSpec((1, 1, s, LANES), slot(3 * heads)),
                  pl.BlockSpec((1, 1, s, LANES), slot(4 * heads)),
                  pl.BlockSpec((1, 1, SUBLANES, l_all), lambda i, h: (i, h, 0, 0)),
                  pl.BlockSpec((1, SUBLANES, LANES), lambda i, h: (h, 0, 0)),
                  pl.BlockSpec((1, SUBLANES, LANES), lambda i, h: (h, 0, 0)),
                  pl.BlockSpec((4, LANES), cw(0)),
                  pl.BlockSpec((4, LANES), cw(heads)),
                  pl.BlockSpec((4, LANES), cw(2 * heads)),
                  pl.BlockSpec((1, LANES), lambda i, h: (0, 0))],
        out_specs=pl.BlockSpec((1, s, LANES), lambda i, h: (i, 0, h)),
        out_shape=jax.ShapeDtypeStruct((b, s, heads * LANES), BF16),
        scratch_shapes=[pltpu.VMEM((l_all, LANES), F32),
                        pltpu.VMEM((l_all, LANES), F32),
                        pltpu.VMEM((l_all, LANES), F32),
                        pltpu.VMEM((l_all, LANES), F32),
                        pltpu.VMEM((SUBLANES, l_all), F32),
                        pltpu.VMEM((N_DIR, l_all, LANES), F32),
                        pltpu.VMEM((N_DIR, l_all, LANES), BF16),
                        pltpu.VMEM((N_DIR, l_all, LANES), BF16),
                        pltpu.VMEM((N_DIR, nct, LANES, CHUNK), BF16),
                        pltpu.VMEM((l_all, LANES), BF16),
                        pltpu.VMEM((N_DIR, LANES, LANES), F32),
                        pltpu.VMEM((s, LANES), F32)],
        compiler_params=_cparams(("parallel", "arbitrary")),
        name="gdn",
    )(pc, pc, pc, plat, plat, plat, plat, grow, alog8, dtb8, conv_w, conv_w, conv_w, norm_w)


def _out_kernel(hs_ref, y_ref, gdn_ref, w_ref, x_ref, gm_ref, scf_ref, shf_ref, nw1_ref, nw2_ref,
                x1_ref, h2_ref, *, heads):
    half = heads * LANES
    y = jnp.concatenate([y_ref[0, j] for j in range(heads)], axis=-1).astype(F32)
    lru = (hs_ref[0] * _gelu_tanh(y)).astype(BF16)
    m = _dot(lru, w_ref[0:half, :]) + _dot(gdn_ref[0], w_ref[half:, :])
    x1 = x_ref[0] + gm_ref[0] * (_rms_rows(m) * nw1_ref[...])
    x1_ref[0] = x1
    h2_ref[0] = (_rms_rows(x1) * nw2_ref[...] * (1.0 + scf_ref[0]) + shf_ref[0]).astype(BF16)


def _out_proj(hs, plat, gdn, w_out, x, gm, scf, shf, nw1, nw2, tm):
    b, s, d = x.shape
    heads = LRU_HEADS
    dm = w_out.shape[0]
    row = lambda i, m: (i, m, 0)
    vec = lambda i, m: (i, 0, 0)
    fix = lambda i, m: (0, 0)
    return pl.pallas_call(
        functools.partial(_out_kernel, heads=heads),
        grid=(b, s // tm),
        in_specs=[pl.BlockSpec((1, tm, heads * LANES), row),
                  pl.BlockSpec((1, heads, tm, LANES), lambda i, m: (i, 0, m, 0)),
                  pl.BlockSpec((1, tm, dm - heads * LANES), row),
                  pl.BlockSpec((dm, d), fix),
                  pl.BlockSpec((1, tm, d), row),
                  pl.BlockSpec((1, 1, d), vec),
                  pl.BlockSpec((1, 1, d), vec),
                  pl.BlockSpec((1, 1, d), vec),
                  pl.BlockSpec((1, d), fix),
                  pl.BlockSpec((1, d), fix)],
        out_specs=[pl.BlockSpec((1, tm, d), row), pl.BlockSpec((1, tm, d), row)],
        out_shape=[jax.ShapeDtypeStruct((b, s, d), F32), jax.ShapeDtypeStruct((b, s, d), BF16)],
        compiler_params=_cparams(("parallel", "parallel")),
        name="out_proj",
    )(hs, plat, gdn, w_out, x, gm, scf, shf, nw1, nw2)


def _mlp_kernel(h_ref, w1_ref, w2_ref, x1_ref, gf_ref, nw_ref, o_ref):
    f = pl.program_id(2)
    hid = jnp.maximum(_dot(h_ref[0], w1_ref[...]), 0.0)
    part = _dot((hid * hid).astype(BF16), w2_ref[...])

    @pl.when(f == 0)
    def _():
        o_ref[0] = part

    @pl.when(f > 0)
    def _():
        o_ref[0] = o_ref[0] + part

    @pl.when(f == pl.num_programs(2) - 1)
    def _():
        o_ref[0] = x1_ref[0] + gf_ref[0] * (_rms_rows(o_ref[0]) * nw_ref[...])


def _mlp(h2, w1, w2, x1, gf, nw3, tm, tf):
    b, s, d = x1.shape
    ff = w1.shape[1]
    row = lambda i, m, f: (i, m, 0)
    return pl.pallas_call(
        _mlp_kernel,
        grid=(b, s // tm, ff // tf),
        in_specs=[pl.BlockSpec((1, tm, d), row),
                  pl.BlockSpec((d, tf), lambda i, m, f: (0, f)),
                  pl.BlockSpec((tf, d), lambda i, m, f: (f, 0)),
                  pl.BlockSpec((1, tm, d), row),
                  pl.BlockSpec((1, 1, d), lambda i, m, f: (i, 0, 0)),
                  pl.BlockSpec((1, d), lambda i, m, f: (0, 0))],
        out_specs=pl.BlockSpec((1, tm, d), row),
        out_shape=jax.ShapeDtypeStruct((b, s, d), F32),
        compiler_params=_cparams(("parallel", "parallel", "arbitrary")),
        name="mlp",
    )(h2, w1, w2, x1, gf, nw3)


def kernel(x, c, ctx, c_ctx, w_mod, b_mod, norm_w, w_in, lru_conv_w, lru_conv_b, lru_gate_w, lru_gate_b,
           lru_lambda, gdn_conv_w, gdn_a_log, gdn_dt_bias, gdn_norm_w, w_out, w_ff1, w_ff2):
    b, s, d = x.shape
    lc = ctx.shape[1]
    l_all = lc + s
    rows = s // GRID_W
    d_lru = LRU_HEADS * LANES
    d_gdn = GDN_HEADS * LANES
    assert w_mod.shape[0] == 1 and b == SUBLANES and d == d_lru + d_gdn

    cc = jnp.concatenate([c, c_ctx[None], jnp.zeros((2 * SUBLANES - b - 1, d), F32)], axis=0)
    mod = _modulation(cc, w_mod[0], b_mod[0])
    sh_m, sc_m, g_m, sh_f, sc_f, g_f = [mod[:b, i * d:(i + 1) * d] for i in range(6)]
    csh_m = jnp.broadcast_to(mod[b:b + 1, 0:d], (b, d))
    csc_m = jnp.broadcast_to(mod[b:b + 1, d:2 * d], (b, d))
    nw = norm_w[0]
    nw0, nw1, nw2, nw3 = [nw[i:i + 1] for i in range(4)]
    v3 = lambda t: t.reshape(b, 1, d)

    w_in0 = w_in[0]
    y_end = 2 * d_lru
    qkv_end = y_end + 3 * d_gdn
    z_end = qkv_end + d_gdn
    w_x = w_in0[:, :d_lru].astype(BF16)
    w_lat = w_in0[:, d_lru:z_end].astype(BF16)
    w_ctx = w_in0[:, y_end:qkv_end].astype(BF16)
    n_gb = w_in0.shape[1] - z_end
    w_gb = jnp.pad(w_in0[:, z_end:], ((0, 0), (0, LANES - n_gb))).astype(BF16)

    p_lat, gb_lat = _inproj_raster(x, nw0, v3(sc_m), v3(sh_m), w_lat, w_gb, tm=1024, tn=512)
    p_ctx, gb_ctx = _inproj_raster(ctx, nw0, v3(csc_m), v3(csh_m), w_ctx, w_gb, tm=lc, tn=512)
    xl = _inproj_tb(x.reshape(b, rows, GRID_W * d), nw0, sc_m, sh_m, w_x, r=rows, wb=4)
    xc = _inproj_tb(ctx, nw0, csc_m, csh_m, w_x, r=128, wb=1)

    gw = lru_gate_w[0]
    wg = jnp.transpose(gw, (2, 3, 0, 1, 4)).reshape(LRU_HEADS, LANES, 4 * LANES).astype(BF16)
    gbias = jnp.transpose(lru_gate_b[0].reshape(N_DIR, 2, LRU_HEADS, LANES), (2, 0, 1, 3))
    gbias = gbias.reshape(LRU_HEADS, 1, 4 * LANES)
    lam2 = jnp.transpose(lru_lambda[0].reshape(N_DIR, LRU_HEADS, LANES), (1, 0, 2)).reshape(1, 2 * d_lru)
    hs = _lru(xc, xl, lru_conv_w[0], lru_conv_b[0].reshape(1, d_lru), wg, gbias, lam2, b, lc, s)

    gb = jnp.concatenate([gb_ctx, gb_lat], axis=1)[:, :, :n_gb]
    gb = gb.reshape(b, l_all, 2, N_DIR, GDN_HEADS)
    grow = jnp.transpose(gb, (0, 4, 2, 3, 1)).reshape(b, GDN_HEADS, 2 * N_DIR, l_all)
    grow = jnp.pad(grow, ((0, 0), (0, 0), (0, SUBLANES - 2 * N_DIR), (0, 0)))
    def rows8(t):
        t = jnp.transpose(t, (1, 0))[:, :, None]
        t = jnp.pad(t, ((0, 0), (2, SUBLANES - 2 - N_DIR), (0, 0)))
        return jnp.broadcast_to(t, (GDN_HEADS, SUBLANES, LANES)).astype(F32)
    gdn = _gdn(p_ctx, p_lat, grow, rows8(gdn_a_log[0]), rows8(gdn_dt_bias[0]), gdn_conv_w[0],
               gdn_norm_w[0].reshape(1, LANES), lc, s)

    x1, h2 = _out_proj(hs, p_lat, gdn, w_out[0].astype(BF16), x, v3(g_m), v3(sc_f), v3(sh_f),
                       nw1, nw2, tm=512)
    return _mlp(h2, w_ff1[0].astype(BF16), w_ff2[0].astype(BF16), x1, v3(g_f), nw3, tm=512, tf=1024)
```

```python
import functools
import math

import jax
import jax.numpy as jnp
from jax import lax
from jax.experimental import pallas as pl
from jax.experimental.pallas import tpu as pltpu

F32 = jnp.float32
BF16 = jnp.bfloat16

LANES = 128
SUBLANES = 8
VMEM_LIMIT = 56 * 1024 * 1024

GRID_W = 64
NORM_EPS = 1e-6
LRU_C = 8.0
LRU_HEADS = 8
GDN_HEADS = 8
N_DIR = 2
CHUNK = 64
NEG_BIG = -1e30
GDN_HEADS_PER_STEP = 2
GDN_CHUNK_BATCH = 6


def _cparams(sem):
    return pltpu.CompilerParams(dimension_semantics=sem, vmem_limit_bytes=VMEM_LIMIT)


def _dot(a, b):
    return jnp.dot(a, b, preferred_element_type=F32)


def _dot_nt(a, b):
    return lax.dot_general(a, b, (((1,), (1,)), ((), ())), preferred_element_type=F32)


def _sigmoid(t):
    return 1.0 / (1.0 + jnp.exp(-t))


def _softplus(t):
    return jnp.maximum(t, 0.0) + jnp.log(1.0 + jnp.exp(-jnp.abs(t)))


def _silu(t):
    return t * _sigmoid(t)


def _gelu_tanh(t):
    return 0.5 * t * (1.0 + jnp.tanh(math.sqrt(2.0 / math.pi) * (t + 0.044715 * (t * t * t))))


def _rms_rows(t):
    return t * lax.rsqrt(jnp.mean(t * t, axis=-1, keepdims=True) + NORM_EPS)


def _mod_kernel(c_ref, w_ref, b_ref, o_ref):
    s = _silu(c_ref[...])
    o_ref[...] = _dot(s.astype(BF16), w_ref[...].astype(BF16)) + b_ref[...]


def _modulation(cc, w_mod, b_mod):
    rows, d = cc.shape
    n = w_mod.shape[1]
    tn = 1024
    return pl.pallas_call(
        _mod_kernel,
        grid=(n // tn,),
        in_specs=[pl.BlockSpec((rows, d), lambda j: (0, 0)),
                  pl.BlockSpec((d, tn), lambda j: (0, j)),
                  pl.BlockSpec((1, tn), lambda j: (0, j))],
        out_specs=pl.BlockSpec((rows, tn), lambda j: (0, j)),
        out_shape=jax.ShapeDtypeStruct((rows, n), F32),
        compiler_params=_cparams(("arbitrary",)),
        name="mod",
    )(cc, w_mod, b_mod.reshape(1, n))


def _inproj_raster_kernel(x_ref, nw_ref, sc_ref, sh_ref, w_ref, wgb_ref, p_ref, gb_ref, h_scr, *, tn):
    @pl.when(pl.program_id(2) == 0)
    def _():
        a = nw_ref[...] * (1.0 + sc_ref[0])
        h = (_rms_rows(x_ref[0]) * a + sh_ref[0]).astype(BF16)
        h_scr[...] = h
        gb_ref[0] = _dot(h, wgb_ref[...])

    acc = _dot(h_scr[...], w_ref[...])
    for j in range(tn // LANES):
        p_ref[0, j] = acc[:, j * LANES:(j + 1) * LANES].astype(p_ref.dtype)


def _inproj_raster(x, nw, sc, sh, w, wgb, tm, tn):
    b, l, d = x.shape
    n = w.shape[1]
    return pl.pallas_call(
        functools.partial(_inproj_raster_kernel, tn=tn),
        grid=(b, l // tm, n // tn),
        in_specs=[pl.BlockSpec((1, tm, d), lambda i, m, j: (i, m, 0)),
                  pl.BlockSpec((1, d), lambda i, m, j: (0, 0)),
                  pl.BlockSpec((1, 1, d), lambda i, m, j: (i, 0, 0)),
                  pl.BlockSpec((1, 1, d), lambda i, m, j: (i, 0, 0)),
                  pl.BlockSpec((d, tn), lambda i, m, j: (0, j)),
                  pl.BlockSpec((d, LANES), lambda i, m, j: (0, 0))],
        out_specs=[pl.BlockSpec((1, tn // LANES, tm, LANES), lambda i, m, j: (i, j, m, 0)),
                   pl.BlockSpec((1, tm, LANES), lambda i, m, j: (i, m, 0))],
        out_shape=[jax.ShapeDtypeStruct((b, n // LANES, l, LANES), BF16),
                   jax.ShapeDtypeStruct((b, l, LANES), F32)],
        scratch_shapes=[pltpu.VMEM((tm, d), BF16)],
        compiler_params=_cparams(("parallel", "parallel", "arbitrary")),
        name="inproj_raster",
    )(x, nw, sc, sh, w, wgb)


def _inproj_tb_kernel(x_ref, nw_ref, sc_ref, sh_ref, w_ref, o_ref, h_scr, *, r, wb, d, nb):
    per_b = r * wb
    for i in range(nb):
        a = nw_ref[...] * (1.0 + sc_ref[i:i + 1, :])
        s = sh_ref[i:i + 1, :]
        for w in range(wb):
            h = _rms_rows(x_ref[i, :, w * d:(w + 1) * d]) * a + s
            h_scr[(i * wb + w) * r:(i * wb + w + 1) * r, :] = h.astype(BF16)
    acc = _dot(h_scr[...], w_ref[...])
    for j in range(w_ref.shape[1] // LANES):
        for i in range(nb):
            o_ref[j, pl.ds(i, per_b, stride=nb), :] = acc[i * per_b:(i + 1) * per_b, j * LANES:(j + 1) * LANES]


def _inproj_tb(xv, nw, sc, sh, w, r, wb):
    nb, r_total, wd = xv.shape
    d = w.shape[0]
    n = w.shape[1]
    assert r_total % r == 0 and (wd // d) % wb == 0
    if r_total != r:
        assert wb == 1 and wd == d
        grid = (r_total // r,)
        x_spec = pl.BlockSpec((nb, r, d), lambda m: (0, m, 0))
    else:
        grid = (wd // (wb * d),)
        x_spec = pl.BlockSpec((nb, r, wb * d), lambda m: (0, 0, m))
    rows_blk = r * wb * nb
    total_rows = r_total * (wd // d) * nb
    return pl.pallas_call(
        functools.partial(_inproj_tb_kernel, r=r, wb=wb, d=d, nb=nb),
        grid=grid,
        in_specs=[x_spec,
                  pl.BlockSpec((1, d), lambda m: (0, 0)),
                  pl.BlockSpec((nb, d), lambda m: (0, 0)),
                  pl.BlockSpec((nb, d), lambda m: (0, 0)),
                  pl.BlockSpec((d, n), lambda m: (0, 0))],
        out_specs=pl.BlockSpec((n // LANES, rows_blk, LANES), lambda m: (0, m, 0)),
        out_shape=jax.ShapeDtypeStruct((n // LANES, total_rows, LANES), F32),
        scratch_shapes=[pltpu.VMEM((rows_blk, d), BF16)],
        compiler_params=_cparams(("parallel",)),
        name="inproj_tb",
    )(xv, nw, sc, sh, w)


def _lru_kernel(xc_ref, xl_ref, cw_ref, cb_ref, wg_ref, gbias_ref, lam_ref, o_ref,
                hsum, af, uf, ab, ub, *, lc, s, rows, tseg, nb):
    hd = LANES
    pitch = rows * nb + SUBLANES
    cols_seg = tseg // rows
    cw = cw_ref[...]
    cb = cb_ref[...]
    sp = _softplus(-lam_ref[...])

    def seg_xr(ref, t0, lseq):
        main = ref[pl.ds(pl.multiple_of(t0 * nb, SUBLANES), tseg * nb), :]
        p0 = jnp.maximum(t0 * nb - 2 * nb, 0)
        prev = ref[pl.ds(pl.multiple_of(p0, SUBLANES), 2 * nb), :]
        prev = jnp.where(t0 > 0, prev, 0.0)
        n0 = jnp.minimum((t0 + tseg) * nb, lseq * nb - nb)
        nxt = ref[pl.ds(pl.multiple_of(n0, SUBLANES), nb), :]
        nxt = jnp.where(t0 + tseg < lseq, nxt, 0.0)
        xm2 = jnp.concatenate([prev, main[:-2 * nb]], axis=0)
        xm1 = jnp.concatenate([prev[nb:], main[:-nb]], axis=0)
        xp1 = jnp.concatenate([main[nb:], nxt], axis=0)
        return cw[0:1] * xm2 + cw[1:2] * xm1 + cw[2:3] * main + cw[3:4] * xp1 + cb

    def gates(xr, direction, a_ref, u_ref):
        lo = direction * 2 * hd
        g = _dot(xr.astype(BF16), wg_ref[0, :, lo:lo + 2 * hd]) + gbias_ref[0, :, lo:lo + 2 * hd]
        rg = _sigmoid(g[:, :hd])
        ig = _sigmoid(g[:, hd:])
        log_a = (-LRU_C) * rg * sp[:, direction * hd:(direction + 1) * hd]
        a_ref[...] = jnp.exp(log_a)
        th = jnp.tanh(log_a)
        u_ref[...] = jnp.sqrt(-2.0 * th / (1.0 - th)) * (ig * xr)

    def scan_segment(hf, hb, store, fseg, bseg):
        def body(i, carry):
            hf, hb = carry
            rf = pl.multiple_of(i * nb, SUBLANES)
            rb = pl.multiple_of((tseg - 1 - i) * nb, SUBLANES)
            hf = af[pl.ds(rf, nb), :] * hf + uf[pl.ds(rf, nb), :]
            hb = ab[pl.ds(rb, nb), :] * hb + ub[pl.ds(rb, nb), :]
            if store:
                jf = fseg * tseg + i
                jb = bseg * tseg + (tseg - 1 - i)
                pf = pl.multiple_of((jf // rows) * pitch + (jf % rows) * nb, SUBLANES)
                pb = pl.multiple_of((jb // rows) * pitch + (jb % rows) * nb, SUBLANES)
                hsum[pl.ds(pf, nb), :] = hsum[pl.ds(pf, nb), :] + hf
                hsum[pl.ds(pb, nb), :] = hsum[pl.ds(pb, nb), :] + hb
            return hf, hb
        return lax.fori_loop(0, tseg, body, (hf, hb))

    hsum[...] = jnp.zeros_like(hsum)
    h0 = jnp.zeros((nb, hd), F32)

    nseg_c = lc // tseg
    def ctx_body(i, carry):
        gates(seg_xr(xc_ref.at[0], i * tseg, lc), 0, af, uf)
        gates(seg_xr(xc_ref.at[0], (nseg_c - 1 - i) * tseg, lc), 1, ab, ub)
        return scan_segment(carry[0], carry[1], False, 0, 0)
    hf, hb = lax.fori_loop(0, nseg_c, ctx_body, (h0, h0))

    nseg_l = s // tseg
    def lat_body(i, carry):
        bseg = nseg_l - 1 - i
        gates(seg_xr(xl_ref.at[0], i * tseg, s), 0, af, uf)
        gates(seg_xr(xl_ref.at[0], bseg * tseg, s), 1, ab, ub)
        return scan_segment(carry[0], carry[1], True, i, bseg)
    lax.fori_loop(0, nseg_l, lat_body, (hf, hb))

    del cols_seg
    def out_body(rr, carry):
        for i in range(nb):
            for wg in range(GRID_W // SUBLANES):
                src = wg * SUBLANES * pitch + rr * nb + i
                o_ref[i, pl.ds(pl.multiple_of(rr * GRID_W + wg * SUBLANES, SUBLANES), SUBLANES), :] = (
                    hsum[pl.ds(src, SUBLANES, stride=pitch), :])
        return carry
    lax.fori_loop(0, rows, out_body, 0)


def _lru(xc, xl, conv_w, conv_b, wg, gbias, lam2, nb, lc, s):
    heads = xl.shape[0]
    rows = s // GRID_W
    tseg = 128
    assert lc % tseg == 0 and s % tseg == 0 and tseg % rows == 0
    pitch = rows * nb + SUBLANES
    return pl.pallas_call(
        functools.partial(_lru_kernel, lc=lc, s=s, rows=rows, tseg=tseg, nb=nb),
        grid=(heads,),
        in_specs=[pl.BlockSpec((1, lc * nb, LANES), lambda h: (h, 0, 0)),
                  pl.BlockSpec((1, s * nb, LANES), lambda h: (h, 0, 0)),
                  pl.BlockSpec((4, LANES), lambda h: (0, h)),
                  pl.BlockSpec((1, LANES), lambda h: (0, h)),
                  pl.BlockSpec((1, LANES, 4 * LANES), lambda h: (h, 0, 0)),
                  pl.BlockSpec((1, 1, 4 * LANES), lambda h: (h, 0, 0)),
                  pl.BlockSpec((1, 2 * LANES), lambda h: (0, h))],
        out_specs=pl.BlockSpec((nb, s, LANES), lambda h: (0, 0, h)),
        out_shape=jax.ShapeDtypeStruct((nb, s, heads * LANES), F32),
        scratch_shapes=[pltpu.VMEM((GRID_W * pitch, LANES), F32)]
                       + [pltpu.VMEM((tseg * nb, LANES), F32)] * 4,
        compiler_params=_cparams(("parallel",)),
        name="lru",
    )(xc, xl, conv_w, conv_b, wg, gbias, lam2)


def _bmm(a, b):
    return lax.dot_general(a, b, (((2,), (1,)), ((0,), (0,))), preferred_element_type=F32)


def _bmm_nt(a, b):
    return lax.dot_general(a, b, (((2,), (2,)), ((0,), (0,))), preferred_element_type=F32)


def _inv_unit_triangular(lm, eye):
    c = lm.shape[-1]
    p = eye - lm
    lb = lm.astype(BF16)
    lk = _bmm(lb, lb)
    n_sq = int(math.log2(c)) - 2
    for _ in range(n_sq):
        lkb = lk.astype(BF16)
        x = _bmm(jnp.concatenate([p.astype(BF16), lkb], axis=1), lkb)
        p = p + x[:, :c]
        lk = x[:, c:]
    return p + _bmm(p.astype(BF16), lk.astype(BF16))


def _gdn_kernel(qc_ref, kc_ref, vc_ref, ql_ref, kl_ref, vl_ref, z_ref, g_ref, alog_ref, dtb_ref,
                cwq_ref, cwk_ref, cwv_ref, nw_ref, o_ref,
                qs, ks, vs, qcol, grow, aq_s, b_s, op_s, st_s, o_s, *, lc, s, hg, nbc):
    dk = LANES
    c = CHUNK
    l_all = lc + s
    ncc = lc // c
    ncl = s // c
    nct = ncc + ncl

    ii = lax.broadcasted_iota(jnp.int32, (c, c), 0)
    jj = lax.broadcasted_iota(jnp.int32, (c, c), 1)
    eye = jnp.where(ii == jj, 1.0, 0.0).astype(F32)
    eye_k = lax.broadcasted_iota(jnp.int32, (dk, dk), 0) == lax.broadcasted_iota(jnp.int32, (dk, dk), 1)
    blk = 2 * LANES
    si = lax.broadcasted_iota(jnp.int32, (blk, blk), 0)
    ji = lax.broadcasted_iota(jnp.int32, (blk, blk), 1)
    same = (si // c) == (ji // c)
    t_pre = jnp.where(same & (si <= ji), 1.0, 0.0).astype(F32)
    t_suf = jnp.where(same & (si >= ji), 1.0, 0.0).astype(F32)

    def conv_silu(ref, hd, cw, n):
        x = ref[0, hd].astype(F32)
        rid = lax.broadcasted_iota(jnp.int32, (n, dk), 0)
        xm2 = jnp.where(rid >= 2, pltpu.roll(x, 2, 0), 0.0)
        xm1 = jnp.where(rid >= 1, pltpu.roll(x, 1, 0), 0.0)
        xp1 = jnp.where(rid < n - 1, pltpu.roll(x, n - 1, 0), 0.0)
        return _silu(cw[0:1] * xm2 + cw[1:2] * xm1 + cw[2:3] * x + cw[3:4] * xp1)

    def l2n(t):
        return t * lax.rsqrt(jnp.sum(t * t, axis=-1, keepdims=True) + NORM_EPS)

    for hd in range(hg):
        cwq = cwq_ref[:, hd * dk:(hd + 1) * dk]
        cwk = cwk_ref[:, hd * dk:(hd + 1) * dk]
        cwv = cwv_ref[:, hd * dk:(hd + 1) * dk]
        qs[0:lc] = l2n(conv_silu(qc_ref, hd, cwq, lc)) * (dk ** -0.5)
        qs[lc:l_all] = l2n(conv_silu(ql_ref, hd, cwq, s)) * (dk ** -0.5)
        ks[0:lc] = l2n(conv_silu(kc_ref, hd, cwk, lc))
        ks[lc:l_all] = l2n(conv_silu(kl_ref, hd, cwk, s))
        vs[0:lc] = conv_silu(vc_ref, hd, cwv, lc)
        vs[lc:l_all] = conv_silu(vl_ref, hd, cwv, s)

        r = g_ref[0, hd]
        rid8 = lax.broadcasted_iota(jnp.int32, (SUBLANES, l_all), 0)
        aneg = -jnp.exp(alog_ref[hd][:, 0:1])
        dtb = dtb_ref[hd][:, 0:1]
        val = jnp.where(rid8 < 2, _sigmoid(r), aneg * _softplus(r + dtb))
        for i in range(l_all // blk):
            vb = val[:, i * blk:(i + 1) * blk]
            pre = jnp.dot(vb, t_pre, preferred_element_type=F32, precision=lax.Precision.HIGHEST)
            suf = jnp.dot(vb, t_suf, preferred_element_type=F32, precision=lax.Precision.HIGHEST)
            rb = lax.broadcasted_iota(jnp.int32, (SUBLANES, blk), 0)
            grow[:, i * blk:(i + 1) * blk] = jnp.where(rb == 2, pre, jnp.where(rb == 3, suf, vb))
        gfull = jnp.concatenate([grow[...], jnp.zeros((LANES - SUBLANES, l_all), F32)], axis=0)
        qcol[...] = gfull.T

        def intra(it, carry, hd=hd):
            r0 = pl.multiple_of(it * (nbc * c), 2 * c)
            q = qs[pl.ds(r0, nbc * c), :].reshape(nbc, c, dk)
            k = ks[pl.ds(r0, nbc * c), :].reshape(nbc, c, dk)
            v = vs[pl.ds(r0, nbc * c), :].reshape(nbc, c, dk)
            col = qcol[pl.ds(r0, nbc * c), :].reshape(nbc, c, LANES)
            growb = grow[:, pl.ds(r0, nbc * c)]
            kb16 = k.astype(BF16)
            both = _bmm_nt(jnp.concatenate([q.astype(BF16), kb16], axis=1), kb16)
            qk = both[:, :c]
            kk = both[:, c:]
            lms, rhss, kds, qkds, qds, cds = [], [], [], [], [], []
            for d in range(N_DIR):
                beta = col[:, :, d:d + 1]
                gcol = col[:, :, 2 + d:3 + d]
                grw = jnp.stack([growb[2 + d:3 + d, j * c:(j + 1) * c] for j in range(nbc)], axis=0)
                keep = (ii >= jj) if d == 0 else (ii <= jj)
                strict = (ii > jj) if d == 0 else (ii < jj)
                decay = jnp.exp(jnp.where(keep, gcol - grw, NEG_BIG))
                eg = jnp.exp(gcol)
                glast = gcol[:, c - 1:c, :] if d == 0 else gcol[:, 0:1, :]
                lms.append(jnp.where(strict, beta * kk * decay, 0.0))
                rhss.append(jnp.concatenate([v * beta, (k * beta) * eg], axis=2).astype(BF16))
                kds.append(k * jnp.exp(glast - gcol))
                qkds.append((qk * decay).astype(BF16))
                qds.append(q * eg)
                cds.append(jnp.exp(glast))
            tinv = _inv_unit_triangular(jnp.concatenate(lms, axis=0), eye)
            uw = _bmm(tinv.astype(BF16), jnp.concatenate(rhss, axis=0)).astype(BF16)
            kdt = jnp.swapaxes(jnp.concatenate(kds, axis=0), 1, 2).astype(BF16)
            m = _bmm(jnp.concatenate([kdt, jnp.concatenate(qkds, axis=0)], axis=1), uw)
            a_mat = jnp.where(eye_k, jnp.concatenate(cds, axis=0), 0.0) - m[:, :dk, dk:]
            q_mat = jnp.concatenate(qds, axis=0) - m[:, dk:, dk:]
            aq = jnp.concatenate([a_mat, q_mat], axis=1).astype(BF16)
            bm = m[:, :dk, :dk].astype(BF16)
            om = m[:, dk:, :dk].astype(BF16)
            for d in range(N_DIR):
                aq_s[hd, d, pl.ds(it * nbc, nbc)] = aq[d * nbc:(d + 1) * nbc]
                b_s[hd, d, pl.ds(it * nbc, nbc)] = bm[d * nbc:(d + 1) * nbc]
                op_s[hd, d, pl.ds(it * nbc, nbc)] = om[d * nbc:(d + 1) * nbc]
            return carry
        lax.fori_loop(0, nct // nbc, intra, 0)

    st_s[...] = jnp.zeros_like(st_s)
    o_s[...] = jnp.zeros_like(o_s)

    def recur(t, carry):
        cf = t
        cb = jnp.where(t < ncc, ncc - 1 - t, 2 * ncc + ncl - 1 - t)
        chains = [(hd, d, ci) for hd in range(hg) for d, ci in ((0, cf), (1, cb))]
        loaded = []
        for hd, d, ci in chains:
            ro = pl.multiple_of(jnp.where(t >= ncc, ci * c - lc, s), c)
            loaded.append((aq_s[hd, d, ci], st_s[hd, d], b_s[hd, d, ci], op_s[hd, d, ci],
                           o_s[hd, pl.ds(ro, c), :], ro))
        results = []
        for aq, st, bm, om, o_old, ro in loaded:
            x = _dot(aq, st.astype(BF16))
            results.append((x[:dk] + bm.astype(F32), o_old + x[dk:] + om.astype(F32), ro))
        for (hd, d, ci), (st_new, o_new, ro) in zip(chains, results):
            st_s[hd, d] = st_new
            o_s[hd, pl.ds(ro, c), :] = o_new
        return carry
    lax.fori_loop(0, nct, recur, 0)

    for hd in range(hg):
        z = z_ref[0, hd].astype(F32)
        o_ref[0, :, hd * dk:(hd + 1) * dk] = (
            (_rms_rows(o_s[hd, 0:s]) * nw_ref[...]).astype(F32) * _silu(z)).astype(o_ref.dtype)


def _gdn(pc, plat, grow, alog8, dtb8, conv_w, norm_w, lc, s):
    b = plat.shape[0]
    heads = GDN_HEADS
    l_all = lc + s
    nct = l_all // CHUNK
    assert lc % CHUNK == 0 and s % CHUNK == 0 and l_all % (2 * LANES) == 0
    assert (s // CHUNK) % 2 == 0 and GDN_CHUNK_BATCH % 2 == 0 and nct % GDN_CHUNK_BATCH == 0
    hg = GDN_HEADS_PER_STEP
    assert heads % hg == 0
    def slot(base):
        return lambda i, h: (i, base // hg + h, 0, 0)
    def cw(base):
        return lambda i, h: (0, base // hg + h)
    return pl.pallas_call(
        functools.partial(_gdn_kernel, lc=lc, s=s, hg=hg, nbc=GDN_CHUNK_BATCH),
        grid=(b, heads // hg),
        in_specs=[pl.BlockSpec((1, hg, lc, LANES), slot(0)),
                  pl.BlockSpec((1, hg, lc, LANES), slot(heads)),
                  pl.BlockSpec((1, hg, lc, LANES), slot(2 * heads)),
                  pl.BlockSpec((1, hg, s, LANES), slot(heads)),
                  pl.BlockSpec((1, hg, s, LANES), slot(2 * heads)),
                  pl.BlockSpec((1, hg, s, LANES), slot(3 * heads)),
                  pl.BlockSpec((1, hg, s, LANES), slot(4 * heads)),
                  pl.BlockSpec((1, hg, SUBLANES, l_all), lambda i, h: (i, h, 0, 0)),
                  pl.BlockSpec((hg, SUBLANES, LANES), lambda i, h: (h, 0, 0)),
                  pl.BlockSpec((hg, SUBLANES, LANES), lambda i, h: (h, 0, 0)),
                  pl.BlockSpec((4, hg * LANES), cw(0)),
                  pl.BlockSpec((4, hg * LANES), cw(heads)),
                  pl.BlockSpec((4, hg * LANES), cw(2 * heads)),
                  pl.BlockSpec((1, LANES), lambda i, h: (0, 0))],
        out_specs=pl.BlockSpec((1, s, hg * LANES), lambda i, h: (i, 0, h)),
        out_shape=jax.ShapeDtypeStruct((b, s, heads * LANES), BF16),
        scratch_shapes=[pltpu.VMEM((l_all, LANES), F32),
                        pltpu.VMEM((l_all, LANES), F32),
                        pltpu.VMEM((l_all, LANES), F32),
                        pltpu.VMEM((l_all, LANES), F32),
                        pltpu.VMEM((SUBLANES, l_all), F32),
                        pltpu.VMEM((hg, N_DIR, nct, LANES + CHUNK, LANES), BF16),
                        pltpu.VMEM((hg, N_DIR, nct, LANES, LANES), BF16),
                        pltpu.VMEM((hg, N_DIR, nct, CHUNK, LANES), BF16),
                        pltpu.VMEM((hg, N_DIR, LANES, LANES), F32),
                        pltpu.VMEM((hg, s + CHUNK, LANES), F32)],
        compiler_params=_cparams(("parallel", "arbitrary")),
        name="gdn",
    )(pc, pc, pc, plat, plat, plat, plat, grow, alog8, dtb8, conv_w, conv_w, conv_w, norm_w)


def _out_kernel(hs_ref, y_ref, gdn_ref, w_ref, x_ref, gm_ref, scf_ref, shf_ref, nw1_ref, nw2_ref,
                x1_ref, h2_ref, *, heads):
    half = heads * LANES
    y = jnp.concatenate([y_ref[0, j] for j in range(heads)], axis=-1).astype(F32)
    lru = (hs_ref[0] * _gelu_tanh(y)).astype(BF16)
    m = _dot(lru, w_ref[0:half, :]) + _dot(gdn_ref[0], w_ref[half:, :])
    x1 = x_ref[0] + gm_ref[0] * (_rms_rows(m) * nw1_ref[...])
    x1_ref[0] = x1
    h2_ref[0] = (_rms_rows(x1) * nw2_ref[...] * (1.0 + scf_ref[0]) + shf_ref[0]).astype(BF16)


def _out_proj(hs, plat, gdn, w_out, x, gm, scf, shf, nw1, nw2, tm):
    b, s, d = x.shape
    heads = LRU_HEADS
    dm = w_out.shape[0]
    row = lambda i, m: (i, m, 0)
    vec = lambda i, m: (i, 0, 0)
    fix = lambda i, m: (0, 0)
    return pl.pallas_call(
        functools.partial(_out_kernel, heads=heads),
        grid=(b, s // tm),
        in_specs=[pl.BlockSpec((1, tm, heads * LANES), row),
                  pl.BlockSpec((1, heads, tm, LANES), lambda i, m: (i, 0, m, 0)),
                  pl.BlockSpec((1, tm, dm - heads * LANES), row),
                  pl.BlockSpec((dm, d), fix),
                  pl.BlockSpec((1, tm, d), row),
                  pl.BlockSpec((1, 1, d), vec),
                  pl.BlockSpec((1, 1, d), vec),
                  pl.BlockSpec((1, 1, d), vec),
                  pl.BlockSpec((1, d), fix),
                  pl.BlockSpec((1, d), fix)],
        out_specs=[pl.BlockSpec((1, tm, d), row), pl.BlockSpec((1, tm, d), row)],
        out_shape=[jax.ShapeDtypeStruct((b, s, d), F32), jax.ShapeDtypeStruct((b, s, d), BF16)],
        compiler_params=_cparams(("parallel", "parallel")),
        name="out_proj",
    )(hs, plat, gdn, w_out, x, gm, scf, shf, nw1, nw2)


def _mlp_kernel(h_ref, w1_ref, w2_ref, x1_ref, gf_ref, nw_ref, o_ref):
    f = pl.program_id(2)
    hid = jnp.maximum(_dot(h_ref[0], w1_ref[...]), 0.0)
    part = _dot((hid * hid).astype(BF16), w2_ref[...])

    @pl.when(f == 0)
    def _():
        o_ref[0] = part

    @pl.when(f > 0)
    def _():
        o_ref[0] = o_ref[0] + part

    @pl.when(f == pl.num_programs(2) - 1)
    def _():
        o_ref[0] = x1_ref[0] + gf_ref[0] * (_rms_rows(o_ref[0]) * nw_ref[...])


def _mlp(h2, w1, w2, x1, gf, nw3, tm, tf):
    b, s, d = x1.shape
    ff = w1.shape[1]
    row = lambda i, m, f: (i, m, 0)
    return pl.pallas_call(
        _mlp_kernel,
        grid=(b, s // tm, ff // tf),
        in_specs=[pl.BlockSpec((1, tm, d), row),
                  pl.BlockSpec((d, tf), lambda i, m, f: (0, f)),
                  pl.BlockSpec((tf, d), lambda i, m, f: (f, 0)),
                  pl.BlockSpec((1, tm, d), row),
                  pl.BlockSpec((1, 1, d), lambda i, m, f: (i, 0, 0)),
                  pl.BlockSpec((1, d), lambda i, m, f: (0, 0))],
        out_specs=pl.BlockSpec((1, tm, d), row),
        out_shape=jax.ShapeDtypeStruct((b, s, d), F32),
        compiler_params=_cparams(("parallel", "parallel", "arbitrary")),
        name="mlp",
    )(h2, w1, w2, x1, gf, nw3)


def kernel(x, c, ctx, c_ctx, w_mod, b_mod, norm_w, w_in, lru_conv_w, lru_conv_b, lru_gate_w, lru_gate_b,
           lru_lambda, gdn_conv_w, gdn_a_log, gdn_dt_bias, gdn_norm_w, w_out, w_ff1, w_ff2):
    b, s, d = x.shape
    lc = ctx.shape[1]
    l_all = lc + s
    rows = s // GRID_W
    d_lru = LRU_HEADS * LANES
    d_gdn = GDN_HEADS * LANES
    assert w_mod.shape[0] == 1 and b == SUBLANES and d == d_lru + d_gdn

    cc = jnp.concatenate([c, c_ctx[None], jnp.zeros((2 * SUBLANES - b - 1, d), F32)], axis=0)
    mod = _modulation(cc, w_mod[0], b_mod[0])
    sh_m, sc_m, g_m, sh_f, sc_f, g_f = [mod[:b, i * d:(i + 1) * d] for i in range(6)]
    csh_m = jnp.broadcast_to(mod[b:b + 1, 0:d], (b, d))
    csc_m = jnp.broadcast_to(mod[b:b + 1, d:2 * d], (b, d))
    nw = norm_w[0]
    nw0, nw1, nw2, nw3 = [nw[i:i + 1] for i in range(4)]
    v3 = lambda t: t.reshape(b, 1, d)

    w_in0 = w_in[0]
    y_end = 2 * d_lru
    qkv_end = y_end + 3 * d_gdn
    z_end = qkv_end + d_gdn
    w_x = w_in0[:, :d_lru].astype(BF16)
    w_lat = w_in0[:, d_lru:z_end].astype(BF16)
    w_ctx = w_in0[:, y_end:qkv_end].astype(BF16)
    n_gb = w_in0.shape[1] - z_end
    w_gb = jnp.pad(w_in0[:, z_end:], ((0, 0), (0, LANES - n_gb))).astype(BF16)

    p_lat, gb_lat = _inproj_raster(x, nw0, v3(sc_m), v3(sh_m), w_lat, w_gb, tm=1024, tn=512)
    p_ctx, gb_ctx = _inproj_raster(ctx, nw0, v3(csc_m), v3(csh_m), w_ctx, w_gb, tm=lc, tn=512)
    xl = _inproj_tb(x.reshape(b, rows, GRID_W * d), nw0, sc_m, sh_m, w_x, r=rows, wb=4)
    xc = _inproj_tb(ctx, nw0, csc_m, csh_m, w_x, r=128, wb=1)

    gw = lru_gate_w[0]
    wg = jnp.transpose(gw, (2, 3, 0, 1, 4)).reshape(LRU_HEADS, LANES, 4 * LANES).astype(BF16)
    gbias = jnp.transpose(lru_gate_b[0].reshape(N_DIR, 2, LRU_HEADS, LANES), (2, 0, 1, 3))
    gbias = gbias.reshape(LRU_HEADS, 1, 4 * LANES)
    lam2 = jnp.transpose(lru_lambda[0].reshape(N_DIR, LRU_HEADS, LANES), (1, 0, 2)).reshape(1, 2 * d_lru)
    hs = _lru(xc, xl, lru_conv_w[0], lru_conv_b[0].reshape(1, d_lru), wg, gbias, lam2, b, lc, s)

    gb = jnp.concatenate([gb_ctx, gb_lat], axis=1)[:, :, :n_gb]
    gb = gb.reshape(b, l_all, 2, N_DIR, GDN_HEADS)
    grow = jnp.transpose(gb, (0, 4, 2, 3, 1)).reshape(b, GDN_HEADS, 2 * N_DIR, l_all)
    grow = jnp.pad(grow, ((0, 0), (0, 0), (0, SUBLANES - 2 * N_DIR), (0, 0)))
    def rows8(t):
        t = jnp.transpose(t, (1, 0))[:, :, None]
        t = jnp.pad(t, ((0, 0), (2, SUBLANES - 2 - N_DIR), (0, 0)))
        return jnp.broadcast_to(t, (GDN_HEADS, SUBLANES, LANES)).astype(F32)
    gdn = _gdn(p_ctx, p_lat, grow, rows8(gdn_a_log[0]), rows8(gdn_dt_bias[0]), gdn_conv_w[0],
               gdn_norm_w[0].reshape(1, LANES), lc, s)

    x1, h2 = _out_proj(hs, p_lat, gdn, w_out[0].astype(BF16), x, v3(g_m), v3(sc_f), v3(sh_f),
                       nw1, nw2, tm=512)
    return _mlp(h2, w_ff1[0].astype(BF16), w_ff2[0].astype(BF16), x1, v3(g_f), nw3, tm=512, tf=1024)
```

```python
import functools
import math

import jax
import jax.numpy as jnp
from jax import lax
from jax.experimental import pallas as pl
from jax.experimental.pallas import tpu as pltpu

F32 = jnp.float32
BF16 = jnp.bfloat16

LANES = 128
SUBLANES = 8
VMEM_LIMIT = 56 * 1024 * 1024

GRID_W = 64
NORM_EPS = 1e-6
LRU_C = 8.0
LRU_HEADS = 8
GDN_HEADS = 8
N_DIR = 2
CHUNK = 64
NEG_BIG = -1e30
LRU_SCAN_BLOCK = 8
GDN_HEADS_PER_STEP = 2
GDN_CHUNK_BATCH = 6


def _cparams(sem):
    return pltpu.CompilerParams(dimension_semantics=sem, vmem_limit_bytes=VMEM_LIMIT)


def _dot(a, b):
    return jnp.dot(a, b, preferred_element_type=F32)


def _dot_nt(a, b):
    return lax.dot_general(a, b, (((1,), (1,)), ((), ())), preferred_element_type=F32)


def _sigmoid(t):
    return 0.5 * jnp.tanh(0.5 * t) + 0.5


def _softplus(t):
    return jnp.maximum(t, 0.0) + jnp.log(1.0 + jnp.exp(-jnp.abs(t)))


def _silu(t):
    return t * _sigmoid(t)


def _gelu_tanh(t):
    return 0.5 * t * (1.0 + jnp.tanh(math.sqrt(2.0 / math.pi) * (t + 0.044715 * (t * t * t))))


def _rms_rows(t):
    return t * lax.rsqrt(jnp.mean(t * t, axis=-1, keepdims=True) + NORM_EPS)


def _mod_kernel(c_ref, w_ref, b_ref, o_ref):
    s = _silu(c_ref[...])
    o_ref[...] = _dot(s.astype(BF16), w_ref[...].astype(BF16)) + b_ref[...]


def _modulation(cc, w_mod, b_mod):
    rows, d = cc.shape
    n = w_mod.shape[1]
    tn = 1024
    return pl.pallas_call(
        _mod_kernel,
        grid=(n // tn,),
        in_specs=[pl.BlockSpec((rows, d), lambda j: (0, 0)),
                  pl.BlockSpec((d, tn), lambda j: (0, j)),
                  pl.BlockSpec((1, tn), lambda j: (0, j))],
        out_specs=pl.BlockSpec((rows, tn), lambda j: (0, j)),
        out_shape=jax.ShapeDtypeStruct((rows, n), F32),
        compiler_params=_cparams(("arbitrary",)),
        name="mod",
    )(cc, w_mod, b_mod.reshape(1, n))


def _inproj_raster_kernel(x_ref, nw_ref, sc_ref, sh_ref, w_ref, wgb_ref, p_ref, gb_ref, h_scr, *, tn):
    @pl.when(pl.program_id(2) == 0)
    def _():
        a = nw_ref[...] * (1.0 + sc_ref[0])
        h = (_rms_rows(x_ref[0]) * a + sh_ref[0]).astype(BF16)
        h_scr[...] = h
        gb_ref[0] = _dot(h, wgb_ref[...])

    acc = _dot(h_scr[...], w_ref[...])
    for j in range(tn // LANES):
        p_ref[0, j] = acc[:, j * LANES:(j + 1) * LANES].astype(p_ref.dtype)


def _inproj_raster(x, nw, sc, sh, w, wgb, tm, tn):
    b, l, d = x.shape
    n = w.shape[1]
    return pl.pallas_call(
        functools.partial(_inproj_raster_kernel, tn=tn),
        grid=(b, l // tm, n // tn),
        in_specs=[pl.BlockSpec((1, tm, d), lambda i, m, j: (i, m, 0)),
                  pl.BlockSpec((1, d), lambda i, m, j: (0, 0)),
                  pl.BlockSpec((1, 1, d), lambda i, m, j: (i, 0, 0)),
                  pl.BlockSpec((1, 1, d), lambda i, m, j: (i, 0, 0)),
                  pl.BlockSpec((d, tn), lambda i, m, j: (0, j)),
                  pl.BlockSpec((d, LANES), lambda i, m, j: (0, 0))],
        out_specs=[pl.BlockSpec((1, tn // LANES, tm, LANES), lambda i, m, j: (i, j, m, 0)),
                   pl.BlockSpec((1, tm, LANES), lambda i, m, j: (i, m, 0))],
        out_shape=[jax.ShapeDtypeStruct((b, n // LANES, l, LANES), BF16),
                   jax.ShapeDtypeStruct((b, l, LANES), F32)],
        scratch_shapes=[pltpu.VMEM((tm, d), BF16)],
        compiler_params=_cparams(("parallel", "parallel", "arbitrary")),
        name="inproj_raster",
    )(x, nw, sc, sh, w, wgb)


def _inproj_tb_kernel(x_ref, nw_ref, sc_ref, sh_ref, w_ref, o_ref, h_scr, *, r, wb, d, nb):
    per_b = r * wb
    for i in range(nb):
        a = nw_ref[...] * (1.0 + sc_ref[i:i + 1, :])
        s = sh_ref[i:i + 1, :]
        for w in range(wb):
            h = _rms_rows(x_ref[i, :, w * d:(w + 1) * d]) * a + s
            h_scr[(i * wb + w) * r:(i * wb + w + 1) * r, :] = h.astype(BF16)
    acc = _dot(h_scr[...], w_ref[...])
    for j in range(w_ref.shape[1] // LANES):
        for i in range(nb):
            o_ref[j, pl.ds(i, per_b, stride=nb), :] = acc[i * per_b:(i + 1) * per_b, j * LANES:(j + 1) * LANES]


def _inproj_tb(xv, nw, sc, sh, w, r, wb):
    nb, r_total, wd = xv.shape
    d = w.shape[0]
    n = w.shape[1]
    assert r_total % r == 0 and (wd // d) % wb == 0
    if r_total != r:
        assert wb == 1 and wd == d
        grid = (r_total // r,)
        x_spec = pl.BlockSpec((nb, r, d), lambda m: (0, m, 0))
    else:
        grid = (wd // (wb * d),)
        x_spec = pl.BlockSpec((nb, r, wb * d), lambda m: (0, 0, m))
    rows_blk = r * wb * nb
    total_rows = r_total * (wd // d) * nb
    return pl.pallas_call(
        functools.partial(_inproj_tb_kernel, r=r, wb=wb, d=d, nb=nb),
        grid=grid,
        in_specs=[x_spec,
                  pl.BlockSpec((1, d), lambda m: (0, 0)),
                  pl.BlockSpec((nb, d), lambda m: (0, 0)),
                  pl.BlockSpec((nb, d), lambda m: (0, 0)),
                  pl.BlockSpec((d, n), lambda m: (0, 0))],
        out_specs=pl.BlockSpec((n // LANES, rows_blk, LANES), lambda m: (0, m, 0)),
        out_shape=jax.ShapeDtypeStruct((n // LANES, total_rows, LANES), F32),
        scratch_shapes=[pltpu.VMEM((rows_blk, d), BF16)],
        compiler_params=_cparams(("parallel",)),
        name="inproj_tb",
    )(xv, nw, sc, sh, w)


def _lru_kernel(xc_ref, xl_ref, cw_ref, cb_ref, wg_ref, gbias_ref, lam_ref, o_ref,
                hsum, af, uf, ab, ub, *, lc, s, rows, tseg, nb):
    hd = LANES
    pitch = rows * nb + SUBLANES
    cols_seg = tseg // rows
    cw = cw_ref[...]
    cb = cb_ref[...]
    half_nsp = (-0.5 * LRU_C) * _softplus(-lam_ref[...])

    def seg_xr(ref, t0, lseq):
        main = ref[pl.ds(pl.multiple_of(t0 * nb, SUBLANES), tseg * nb), :]
        p0 = jnp.maximum(t0 * nb - 2 * nb, 0)
        prev = ref[pl.ds(pl.multiple_of(p0, SUBLANES), 2 * nb), :]
        prev = jnp.where(t0 > 0, prev, 0.0)
        n0 = jnp.minimum((t0 + tseg) * nb, lseq * nb - nb)
        nxt = ref[pl.ds(pl.multiple_of(n0, SUBLANES), nb), :]
        nxt = jnp.where(t0 + tseg < lseq, nxt, 0.0)
        xm2 = jnp.concatenate([prev, main[:-2 * nb]], axis=0)
        xm1 = jnp.concatenate([prev[nb:], main[:-nb]], axis=0)
        xp1 = jnp.concatenate([main[nb:], nxt], axis=0)
        return cw[0:1] * xm2 + cw[1:2] * xm1 + cw[2:3] * main + cw[3:4] * xp1 + cb

    def gates(xr, direction, a_ref, u_ref):
        lo = direction * 2 * hd
        th_g = jnp.tanh(_dot(xr.astype(BF16), wg_ref[0, :, lo:lo + 2 * hd]) + gbias_ref[0, :, lo:lo + 2 * hd])
        hnsp = half_nsp[:, direction * hd:(direction + 1) * hd]
        log_a = th_g[:, :hd] * hnsp + hnsp
        ig = 0.5 * th_g[:, hd:] + 0.5
        a = jnp.exp(log_a)
        a_ref[...] = a
        sq = -jnp.tanh(log_a) * (1.0 + a * a)
        root = jnp.where(sq > 0.0, sq * lax.rsqrt(sq), 0.0)
        u_ref[...] = root * (ig * xr)

    def scan_segment(hf, hb, store, fseg, bseg):
        fbase = pl.multiple_of(fseg * cols_seg * pitch, SUBLANES)
        bbase = pl.multiple_of(bseg * cols_seg * pitch, SUBLANES)
        for blk in range(tseg // LRU_SCAN_BLOCK):
            for direction in range(N_DIR):
                a_ref, u_ref = (af, uf) if direction == 0 else (ab, ub)
                h0 = hf if direction == 0 else hb
                base = fbase if direction == 0 else bbase
                pa = pu = None
                for kk in range(LRU_SCAN_BLOCK):
                    step = blk * LRU_SCAN_BLOCK + kk
                    t = step if direction == 0 else tseg - 1 - step
                    a = a_ref[t * nb:(t + 1) * nb, :]
                    u = u_ref[t * nb:(t + 1) * nb, :]
                    pa, pu = (a, u) if kk == 0 else (a * pa, a * pu + u)
                    h = pa * h0 + pu
                    off = (t // rows) * pitch + (t % rows) * nb
                    if store == "set":
                        hsum[pl.ds(base + off, nb), :] = h
                    elif store == "add":
                        hsum[pl.ds(base + off, nb), :] = hsum[pl.ds(base + off, nb), :] + h
                if direction == 0:
                    hf = h
                else:
                    hb = h
        return hf, hb

    h0 = jnp.zeros((nb, hd), F32)

    nseg_c = lc // tseg
    def ctx_body(i, carry):
        gates(seg_xr(xc_ref.at[0], i * tseg, lc), 0, af, uf)
        gates(seg_xr(xc_ref.at[0], (nseg_c - 1 - i) * tseg, lc), 1, ab, ub)
        return scan_segment(carry[0], carry[1], None, 0, 0)
    hf, hb = lax.fori_loop(0, nseg_c, ctx_body, (h0, h0))

    nseg_l = s // tseg
    def lat_body(store):
        def body(i, carry):
            bseg = nseg_l - 1 - i
            gates(seg_xr(xl_ref.at[0], i * tseg, s), 0, af, uf)
            gates(seg_xr(xl_ref.at[0], bseg * tseg, s), 1, ab, ub)
            return scan_segment(carry[0], carry[1], store, i, bseg)
        return body
    hf, hb = lax.fori_loop(0, nseg_l // 2, lat_body("set"), (hf, hb))
    lax.fori_loop(nseg_l // 2, nseg_l, lat_body("add"), (hf, hb))

    def out_body(rr, carry):
        for i in range(nb):
            for wg in range(GRID_W // SUBLANES):
                src = wg * SUBLANES * pitch + rr * nb + i
                o_ref[i, pl.ds(pl.multiple_of(rr * GRID_W + wg * SUBLANES, SUBLANES), SUBLANES), :] = (
                    hsum[pl.ds(src, SUBLANES, stride=pitch), :])
        return carry
    lax.fori_loop(0, rows, out_body, 0)


def _lru(xc, xl, conv_w, conv_b, wg, gbias, lam2, nb, lc, s):
    heads = xl.shape[0]
    rows = s // GRID_W
    tseg = 128
    assert lc % tseg == 0 and s % (2 * tseg) == 0 and tseg % rows == 0 and tseg % LRU_SCAN_BLOCK == 0
    pitch = rows * nb + SUBLANES
    return pl.pallas_call(
        functools.partial(_lru_kernel, lc=lc, s=s, rows=rows, tseg=tseg, nb=nb),
        grid=(heads,),
        in_specs=[pl.BlockSpec((1, lc * nb, LANES), lambda h: (h, 0, 0)),
                  pl.BlockSpec((1, s * nb, LANES), lambda h: (h, 0, 0)),
                  pl.BlockSpec((4, LANES), lambda h: (0, h)),
                  pl.BlockSpec((1, LANES), lambda h: (0, h)),
                  pl.BlockSpec((1, LANES, 4 * LANES), lambda h: (h, 0, 0)),
                  pl.BlockSpec((1, 1, 4 * LANES), lambda h: (h, 0, 0)),
                  pl.BlockSpec((1, 2 * LANES), lambda h: (0, h))],
        out_specs=pl.BlockSpec((nb, s, LANES), lambda h: (0, 0, h)),
        out_shape=jax.ShapeDtypeStruct((nb, s, heads * LANES), F32),
        scratch_shapes=[pltpu.VMEM((GRID_W * pitch, LANES), F32)]
                       + [pltpu.VMEM((tseg * nb, LANES), F32)] * 4,
        compiler_params=_cparams(("parallel",)),
        name="lru",
    )(xc, xl, conv_w, conv_b, wg, gbias, lam2)


def _bmm(a, b):
    return lax.dot_general(a, b, (((2,), (1,)), ((0,), (0,))), preferred_element_type=F32)


def _bmm_nt(a, b):
    return lax.dot_general(a, b, (((2,), (2,)), ((0,), (0,))), preferred_element_type=F32)


def _inv_unit_triangular(lm, eye):
    c = lm.shape[-1]
    p = eye - lm
    lb = lm.astype(BF16)
    lk = _bmm(lb, lb)
    n_sq = int(math.log2(c)) - 2
    for _ in range(n_sq):
        lkb = lk.astype(BF16)
        x = _bmm(jnp.concatenate([p.astype(BF16), lkb], axis=1), lkb)
        p = p + x[:, :c]
        lk = x[:, c:]
    return p + _bmm(p.astype(BF16), lk.astype(BF16))


def _gdn_kernel(qc_ref, kc_ref, vc_ref, ql_ref, kl_ref, vl_ref, z_ref, g_ref, alog_ref, dtb_ref,
                cwq_ref, cwk_ref, cwv_ref, nw_ref, o_ref,
                qs, ks, vs, qcol, grow, aq_s, b_s, op_s, st_s, o_s, *, lc, s, hg, nbc):
    dk = LANES
    c = CHUNK
    l_all = lc + s
    ncc = lc // c
    ncl = s // c
    nct = ncc + ncl

    ii = lax.broadcasted_iota(jnp.int32, (c, c), 0)
    jj = lax.broadcasted_iota(jnp.int32, (c, c), 1)
    eye = jnp.where(ii == jj, 1.0, 0.0).astype(F32)
    eye_k = lax.broadcasted_iota(jnp.int32, (dk, dk), 0) == lax.broadcasted_iota(jnp.int32, (dk, dk), 1)
    blk = 2 * LANES
    si = lax.broadcasted_iota(jnp.int32, (blk, blk), 0)
    ji = lax.broadcasted_iota(jnp.int32, (blk, blk), 1)
    same = (si // c) == (ji // c)
    t_pre = jnp.where(same & (si <= ji), 1.0, 0.0).astype(F32)
    t_suf = jnp.where(same & (si >= ji), 1.0, 0.0).astype(F32)

    def conv_silu(ref, hd, cw, n):
        x = ref[0, hd].astype(F32)
        rid = lax.broadcasted_iota(jnp.int32, (n, dk), 0)
        xm2 = jnp.where(rid >= 2, pltpu.roll(x, 2, 0), 0.0)
        xm1 = jnp.where(rid >= 1, pltpu.roll(x, 1, 0), 0.0)
        xp1 = jnp.where(rid < n - 1, pltpu.roll(x, n - 1, 0), 0.0)
        return _silu(cw[0:1] * xm2 + cw[1:2] * xm1 + cw[2:3] * x + cw[3:4] * xp1)

    def l2n(t):
        return t * lax.rsqrt(jnp.sum(t * t, axis=-1, keepdims=True) + NORM_EPS)

    for hd in range(hg):
        cwq = cwq_ref[:, hd * dk:(hd + 1) * dk]
        cwk = cwk_ref[:, hd * dk:(hd + 1) * dk]
        cwv = cwv_ref[:, hd * dk:(hd + 1) * dk]
        qs[0:lc] = l2n(conv_silu(qc_ref, hd, cwq, lc)) * (dk ** -0.5)
        qs[lc:l_all] = l2n(conv_silu(ql_ref, hd, cwq, s)) * (dk ** -0.5)
        ks[0:lc] = l2n(conv_silu(kc_ref, hd, cwk, lc))
        ks[lc:l_all] = l2n(conv_silu(kl_ref, hd, cwk, s))
        vs[0:lc] = conv_silu(vc_ref, hd, cwv, lc)
        vs[lc:l_all] = conv_silu(vl_ref, hd, cwv, s)

        r = g_ref[0, hd]
        rid8 = lax.broadcasted_iota(jnp.int32, (SUBLANES, l_all), 0)
        aneg = -jnp.exp(alog_ref[hd][:, 0:1])
        dtb = dtb_ref[hd][:, 0:1]
        val = jnp.where(rid8 < 2, _sigmoid(r), aneg * _softplus(r + dtb))
        for i in range(l_all // blk):
            vb = val[:, i * blk:(i + 1) * blk]
            pre = jnp.dot(vb, t_pre, preferred_element_type=F32, precision=lax.Precision.HIGHEST)
            suf = jnp.dot(vb, t_suf, preferred_element_type=F32, precision=lax.Precision.HIGHEST)
            rb = lax.broadcasted_iota(jnp.int32, (SUBLANES, blk), 0)
            grow[:, i * blk:(i + 1) * blk] = jnp.where(rb == 2, pre, jnp.where(rb == 3, suf, vb))
        gfull = jnp.concatenate([grow[...], jnp.zeros((LANES - SUBLANES, l_all), F32)], axis=0)
        qcol[...] = gfull.T

        def intra(it, carry, hd=hd):
            r0 = pl.multiple_of(it * (nbc * c), 2 * c)
            q = qs[pl.ds(r0, nbc * c), :].reshape(nbc, c, dk)
            k = ks[pl.ds(r0, nbc * c), :].reshape(nbc, c, dk)
            v = vs[pl.ds(r0, nbc * c), :].reshape(nbc, c, dk)
            col = qcol[pl.ds(r0, nbc * c), :].reshape(nbc, c, LANES)
            growb = grow[:, pl.ds(r0, nbc * c)]
            kb16 = k.astype(BF16)
            both = _bmm_nt(jnp.concatenate([q.astype(BF16), kb16], axis=1), kb16)
            qk = both[:, :c]
            kk = both[:, c:]
            lms, rhss, kds, qkds, qds, cds = [], [], [], [], [], []
            for d in range(N_DIR):
                beta = col[:, :, d:d + 1]
                gcol = col[:, :, 2 + d:3 + d]
                grw = jnp.stack([growb[2 + d:3 + d, j * c:(j + 1) * c] for j in range(nbc)], axis=0)
                keep = (ii >= jj) if d == 0 else (ii <= jj)
                strict = (ii > jj) if d == 0 else (ii < jj)
                decay = jnp.exp(jnp.where(keep, gcol - grw, NEG_BIG))
                eg = jnp.exp(gcol)
                glast = gcol[:, c - 1:c, :] if d == 0 else gcol[:, 0:1, :]
                lms.append(jnp.where(strict, beta * kk * decay, 0.0))
                rhss.append(jnp.concatenate([v * beta, (k * beta) * eg], axis=2).astype(BF16))
                kds.append(k * jnp.exp(glast - gcol))
                qkds.append((qk * decay).astype(BF16))
                qds.append(q * eg)
                cds.append(jnp.exp(glast))
            tinv = _inv_unit_triangular(jnp.concatenate(lms, axis=0), eye)
            uw = _bmm(tinv.astype(BF16), jnp.concatenate(rhss, axis=0)).astype(BF16)
            kdt = jnp.swapaxes(jnp.concatenate(kds, axis=0), 1, 2).astype(BF16)
            m = _bmm(jnp.concatenate([kdt, jnp.concatenate(qkds, axis=0)], axis=1), uw)
            a_mat = jnp.where(eye_k, jnp.concatenate(cds, axis=0), 0.0) - m[:, :dk, dk:]
            q_mat = jnp.concatenate(qds, axis=0) - m[:, dk:, dk:]
            aq = jnp.concatenate([a_mat, q_mat], axis=1).astype(BF16)
            bm = m[:, :dk, :dk].astype(BF16)
            om = m[:, dk:, :dk].astype(BF16)
            for d in range(N_DIR):
                aq_s[hd, d, pl.ds(it * nbc, nbc)] = aq[d * nbc:(d + 1) * nbc]
                b_s[hd, d, pl.ds(it * nbc, nbc)] = bm[d * nbc:(d + 1) * nbc]
                op_s[hd, d, pl.ds(it * nbc, nbc)] = om[d * nbc:(d + 1) * nbc]
            return carry
        lax.fori_loop(0, nct // nbc, intra, 0)

    st_s[...] = jnp.zeros_like(st_s)
    o_s[...] = jnp.zeros_like(o_s)

    def recur(t, carry):
        cf = t
        cb = jnp.where(t < ncc, ncc - 1 - t, 2 * ncc + ncl - 1 - t)
        chains = [(hd, d, ci) for hd in range(hg) for d, ci in ((0, cf), (1, cb))]
        loaded = []
        for hd, d, ci in chains:
            ro = pl.multiple_of(jnp.where(t >= ncc, ci * c - lc, s), c)
            loaded.append((aq_s[hd, d, ci], st_s[hd, d], b_s[hd, d, ci], op_s[hd, d, ci],
                           o_s[hd, pl.ds(ro, c), :], ro))
        results = []
        for aq, st, bm, om, o_old, ro in loaded:
            x = _dot(aq, st.astype(BF16))
            results.append((x[:dk] + bm.astype(F32), o_old + x[dk:] + om.astype(F32), ro))
        for (hd, d, ci), (st_new, o_new, ro) in zip(chains, results):
            st_s[hd, d] = st_new
            o_s[hd, pl.ds(ro, c), :] = o_new
        return carry
    lax.fori_loop(0, nct, recur, 0)

    for hd in range(hg):
        z = z_ref[0, hd].astype(F32)
        o_ref[0, :, hd * dk:(hd + 1) * dk] = (
            (_rms_rows(o_s[hd, 0:s]) * nw_ref[...]).astype(F32) * _silu(z)).astype(o_ref.dtype)


def _gdn(pc, plat, grow, alog8, dtb8, conv_w, norm_w, lc, s):
    b = plat.shape[0]
    heads = GDN_HEADS
    l_all = lc + s
    nct = l_all // CHUNK
    assert lc % CHUNK == 0 and s % CHUNK == 0 and l_all % (2 * LANES) == 0
    assert (s // CHUNK) % 2 == 0 and GDN_CHUNK_BATCH % 2 == 0 and nct % GDN_CHUNK_BATCH == 0
    hg = GDN_HEADS_PER_STEP
    assert heads % hg == 0
    def slot(base):
        return lambda i, h: (i, base // hg + h, 0, 0)
    def cw(base):
        return lambda i, h: (0, base // hg + h)
    return pl.pallas_call(
        functools.partial(_gdn_kernel, lc=lc, s=s, hg=hg, nbc=GDN_CHUNK_BATCH),
        grid=(b, heads // hg),
        in_specs=[pl.BlockSpec((1, hg, lc, LANES), slot(0)),
                  pl.BlockSpec((1, hg, lc, LANES), slot(heads)),
                  pl.BlockSpec((1, hg, lc, LANES), slot(2 * heads)),
                  pl.BlockSpec((1, hg, s, LANES), slot(heads)),
                  pl.BlockSpec((1, hg, s, LANES), slot(2 * heads)),
                  pl.BlockSpec((1, hg, s, LANES), slot(3 * heads)),
                  pl.BlockSpec((1, hg, s, LANES), slot(4 * heads)),
                  pl.BlockSpec((1, hg, SUBLANES, l_all), lambda i, h: (i, h, 0, 0)),
                  pl.BlockSpec((hg, SUBLANES, LANES), lambda i, h: (h, 0, 0)),
                  pl.BlockSpec((hg, SUBLANES, LANES), lambda i, h: (h, 0, 0)),
                  pl.BlockSpec((4, hg * LANES), cw(0)),
                  pl.BlockSpec((4, hg * LANES), cw(heads)),
                  pl.BlockSpec((4, hg * LANES), cw(2 * heads)),
                  pl.BlockSpec((1, LANES), lambda i, h: (0, 0))],
        out_specs=pl.BlockSpec((1, s, hg * LANES), lambda i, h: (i, 0, h)),
        out_shape=jax.ShapeDtypeStruct((b, s, heads * LANES), BF16),
        scratch_shapes=[pltpu.VMEM((l_all, LANES), F32),
                        pltpu.VMEM((l_all, LANES), F32),
                        pltpu.VMEM((l_all, LANES), F32),
                        pltpu.VMEM((l_all, LANES), F32),
                        pltpu.VMEM((SUBLANES, l_all), F32),
                        pltpu.VMEM((hg, N_DIR, nct, LANES + CHUNK, LANES), BF16),
                        pltpu.VMEM((hg, N_DIR, nct, LANES, LANES), BF16),
                        pltpu.VMEM((hg, N_DIR, nct, CHUNK, LANES), BF16),
                        pltpu.VMEM((hg, N_DIR, LANES, LANES), F32),
                        pltpu.VMEM((hg, s + CHUNK, LANES), F32)],
        compiler_params=_cparams(("parallel", "arbitrary")),
        name="gdn",
    )(pc, pc, pc, plat, plat, plat, plat, grow, alog8, dtb8, conv_w, conv_w, conv_w, norm_w)


def _out_kernel(hs_ref, y_ref, gdn_ref, w_ref, x_ref, gm_ref, scf_ref, shf_ref, nw1_ref, nw2_ref,
                x1_ref, h2_ref, *, heads):
    half = heads * LANES
    y = jnp.concatenate([y_ref[0, j] for j in range(heads)], axis=-1).astype(F32)
    lru = (hs_ref[0] * _gelu_tanh(y)).astype(BF16)
    m = _dot(lru, w_ref[0:half, :]) + _dot(gdn_ref[0], w_ref[half:, :])
    x1 = x_ref[0] + gm_ref[0] * (_rms_rows(m) * nw1_ref[...])
    x1_ref[0] = x1
    h2_ref[0] = (_rms_rows(x1) * nw2_ref[...] * (1.0 + scf_ref[0]) + shf_ref[0]).astype(BF16)


def _out_proj(hs, plat, gdn, w_out, x, gm, scf, shf, nw1, nw2, tm):
    b, s, d = x.shape
    heads = LRU_HEADS
    dm = w_out.shape[0]
    row = lambda i, m: (i, m, 0)
    vec = lambda i, m: (i, 0, 0)
    fix = lambda i, m: (0, 0)
    return pl.pallas_call(
        functools.partial(_out_kernel, heads=heads),
        grid=(b, s // tm),
        in_specs=[pl.BlockSpec((1, tm, heads * LANES), row),
                  pl.BlockSpec((1, heads, tm, LANES), lambda i, m: (i, 0, m, 0)),
                  pl.BlockSpec((1, tm, dm - heads * LANES), row),
                  pl.BlockSpec((dm, d), fix),
                  pl.BlockSpec((1, tm, d), row),
                  pl.BlockSpec((1, 1, d), vec),
                  pl.BlockSpec((1, 1, d), vec),
                  pl.BlockSpec((1, 1, d), vec),
                  pl.BlockSpec((1, d), fix),
                  pl.BlockSpec((1, d), fix)],
        out_specs=[pl.BlockSpec((1, tm, d), row), pl.BlockSpec((1, tm, d), row)],
        out_shape=[jax.ShapeDtypeStruct((b, s, d), F32), jax.ShapeDtypeStruct((b, s, d), BF16)],
        compiler_params=_cparams(("parallel", "parallel")),
        name="out_proj",
    )(hs, plat, gdn, w_out, x, gm, scf, shf, nw1, nw2)


def _mlp_kernel(h_ref, w1_ref, w2_ref, x1_ref, gf_ref, nw_ref, o_ref):
    f = pl.program_id(2)

    @pl.when(f == 0)
    def _():
        o_ref[0] = jnp.zeros_like(o_ref[0])

    hid = jnp.maximum(_dot(h_ref[0], w1_ref[...]), 0.0)
    o_ref[0] = o_ref[0] + _dot((hid * hid).astype(BF16), w2_ref[...])

    @pl.when(f == pl.num_programs(2) - 1)
    def _():
        o_ref[0] = x1_ref[0] + gf_ref[0] * (_rms_rows(o_ref[0]) * nw_ref[...])


def _mlp(h2, w1, w2, x1, gf, nw3, tm, tf):
    b, s, d = x1.shape
    ff = w1.shape[1]
    row = lambda i, m, f: (i, m, 0)
    return pl.pallas_call(
        _mlp_kernel,
        grid=(b, s // tm, ff // tf),
        in_specs=[pl.BlockSpec((1, tm, d), row),
                  pl.BlockSpec((d, tf), lambda i, m, f: (0, f)),
                  pl.BlockSpec((tf, d), lambda i, m, f: (f, 0)),
                  pl.BlockSpec((1, tm, d), row),
                  pl.BlockSpec((1, 1, d), lambda i, m, f: (i, 0, 0)),
                  pl.BlockSpec((1, d), lambda i, m, f: (0, 0))],
        out_specs=pl.BlockSpec((1, tm, d), row),
        out_shape=jax.ShapeDtypeStruct((b, s, d), F32),
        compiler_params=_cparams(("parallel", "parallel", "arbitrary")),
        name="mlp",
    )(h2, w1, w2, x1, gf, nw3)


def kernel(x, c, ctx, c_ctx, w_mod, b_mod, norm_w, w_in, lru_conv_w, lru_conv_b, lru_gate_w, lru_gate_b,
           lru_lambda, gdn_conv_w, gdn_a_log, gdn_dt_bias, gdn_norm_w, w_out, w_ff1, w_ff2):
    b, s, d = x.shape
    lc = ctx.shape[1]
    l_all = lc + s
    rows = s // GRID_W
    d_lru = LRU_HEADS * LANES
    d_gdn = GDN_HEADS * LANES
    assert w_mod.shape[0] == 1 and b == SUBLANES and d == d_lru + d_gdn

    cc = jnp.concatenate([c, c_ctx[None], jnp.zeros((2 * SUBLANES - b - 1, d), F32)], axis=0)
    mod = _modulation(cc, w_mod[0], b_mod[0])
    sh_m, sc_m, g_m, sh_f, sc_f, g_f = [mod[:b, i * d:(i + 1) * d] for i in range(6)]
    csh_m = jnp.broadcast_to(mod[b:b + 1, 0:d], (b, d))
    csc_m = jnp.broadcast_to(mod[b:b + 1, d:2 * d], (b, d))
    nw = norm_w[0]
    nw0, nw1, nw2, nw3 = [nw[i:i + 1] for i in range(4)]
    v3 = lambda t: t.reshape(b, 1, d)

    w_in0 = w_in[0]
    y_end = 2 * d_lru
    qkv_end = y_end + 3 * d_gdn
    z_end = qkv_end + d_gdn
    w_x = w_in0[:, :d_lru].astype(BF16)
    w_lat = w_in0[:, d_lru:z_end].astype(BF16)
    w_ctx = w_in0[:, y_end:qkv_end].astype(BF16)
    n_gb = w_in0.shape[1] - z_end
    w_gb = jnp.pad(w_in0[:, z_end:], ((0, 0), (0, LANES - n_gb))).astype(BF16)

    p_lat, gb_lat = _inproj_raster(x, nw0, v3(sc_m), v3(sh_m), w_lat, w_gb, tm=1024, tn=512)
    p_ctx, gb_ctx = _inproj_raster(ctx, nw0, v3(csc_m), v3(csh_m), w_ctx, w_gb, tm=lc, tn=512)
    xl = _inproj_tb(x.reshape(b, rows, GRID_W * d), nw0, sc_m, sh_m, w_x, r=rows, wb=4)
    xc = _inproj_tb(ctx, nw0, csc_m, csh_m, w_x, r=128, wb=1)

    gw = lru_gate_w[0]
    wg = (0.5 * jnp.transpose(gw, (2, 3, 0, 1, 4))).reshape(LRU_HEADS, LANES, 4 * LANES).astype(BF16)
    gbias = 0.5 * jnp.transpose(lru_gate_b[0].reshape(N_DIR, 2, LRU_HEADS, LANES), (2, 0, 1, 3))
    gbias = gbias.reshape(LRU_HEADS, 1, 4 * LANES)
    lam2 = jnp.transpose(lru_lambda[0].reshape(N_DIR, LRU_HEADS, LANES), (1, 0, 2)).reshape(1, 2 * d_lru)
    hs = _lru(xc, xl, lru_conv_w[0], lru_conv_b[0].reshape(1, d_lru), wg, gbias, lam2, b, lc, s)

    gb = jnp.concatenate([gb_ctx, gb_lat], axis=1)[:, :, :n_gb]
    gb = gb.reshape(b, l_all, 2, N_DIR, GDN_HEADS)
    grow = jnp.transpose(gb, (0, 4, 2, 3, 1)).reshape(b, GDN_HEADS, 2 * N_DIR, l_all)
    grow = jnp.pad(grow, ((0, 0), (0, 0), (0, SUBLANES - 2 * N_DIR), (0, 0)))
    def rows8(t):
        t = jnp.transpose(t, (1, 0))[:, :, None]
        t = jnp.pad(t, ((0, 0), (2, SUBLANES - 2 - N_DIR), (0, 0)))
        return jnp.broadcast_to(t, (GDN_HEADS, SUBLANES, LANES)).astype(F32)
    gdn = _gdn(p_ctx, p_lat, grow, rows8(gdn_a_log[0]), rows8(gdn_dt_bias[0]), gdn_conv_w[0],
               gdn_norm_w[0].reshape(1, LANES), lc, s)

    x1, h2 = _out_proj(hs, p_lat, gdn, w_out[0].astype(BF16), x, v3(g_m), v3(sc_f), v3(sh_f),
                       nw1, nw2, tm=512)
    return _mlp(h2, w_ff1[0].astype(BF16), w_ff2[0].astype(BF16), x1, v3(g_f), nw3, tm=512, tf=1024)
```

```python
import functools
import math

import jax
import jax.numpy as jnp
from jax import lax
from jax.experimental import pallas as pl
from jax.experimental.pallas import tpu as pltpu

F32 = jnp.float32
BF16 = jnp.bfloat16

LANES = 128
SUBLANES = 8
VMEM_LIMIT = 56 * 1024 * 1024

GRID_W = 64
NORM_EPS = 1e-6
LRU_C = 8.0
LRU_HEADS = 8
GDN_HEADS = 8
N_DIR = 2
CHUNK = 64
NEG_BIG = -1e30
LRU_SCAN_BLOCK = 8
GDN_HEADS_PER_STEP = 2
GDN_CHUNK_BATCH = 18


def _cparams(sem):
    return pltpu.CompilerParams(dimension_semantics=sem, vmem_limit_bytes=VMEM_LIMIT)


def _dot(a, b):
    return jnp.dot(a, b, preferred_element_type=F32)


def _dot_nt(a, b):
    return lax.dot_general(a, b, (((1,), (1,)), ((), ())), preferred_element_type=F32)


def _sigmoid(t):
    return 0.5 * jnp.tanh(0.5 * t) + 0.5


def _softplus(t):
    return jnp.maximum(t, 0.0) + jnp.log(1.0 + jnp.exp(-jnp.abs(t)))


def _silu(t):
    return t * _sigmoid(t)


def _gelu_tanh(t):
    return 0.5 * t * (1.0 + jnp.tanh(math.sqrt(2.0 / math.pi) * (t + 0.044715 * (t * t * t))))


def _rms_rows(t):
    return t * lax.rsqrt(jnp.mean(t * t, axis=-1, keepdims=True) + NORM_EPS)


def _mod_kernel(c_ref, w_ref, b_ref, o_ref):
    s = _silu(c_ref[...])
    o_ref[...] = _dot(s.astype(BF16), w_ref[...].astype(BF16)) + b_ref[...]


def _modulation(cc, w_mod, b_mod):
    rows, d = cc.shape
    n = w_mod.shape[1]
    tn = 1024
    return pl.pallas_call(
        _mod_kernel,
        grid=(n // tn,),
        in_specs=[pl.BlockSpec((rows, d), lambda j: (0, 0)),
                  pl.BlockSpec((d, tn), lambda j: (0, j)),
                  pl.BlockSpec((1, tn), lambda j: (0, j))],
        out_specs=pl.BlockSpec((rows, tn), lambda j: (0, j)),
        out_shape=jax.ShapeDtypeStruct((rows, n), F32),
        compiler_params=_cparams(("arbitrary",)),
        name="mod",
    )(cc, w_mod, b_mod.reshape(1, n))


def _inproj_raster_kernel(x_ref, nw_ref, sc_ref, sh_ref, w_ref, wgb_ref, p_ref, gb_ref, h_scr, *, tn):
    @pl.when(pl.program_id(2) == 0)
    def _():
        a = nw_ref[...] * (1.0 + sc_ref[0])
        h = (_rms_rows(x_ref[0]) * a + sh_ref[0]).astype(BF16)
        h_scr[...] = h
        gb_ref[0] = _dot(h, wgb_ref[...])

    acc = _dot(h_scr[...], w_ref[...])
    for j in range(tn // LANES):
        p_ref[0, j] = acc[:, j * LANES:(j + 1) * LANES].astype(p_ref.dtype)


def _inproj_raster(x, nw, sc, sh, w, wgb, tm, tn):
    b, l, d = x.shape
    n = w.shape[1]
    return pl.pallas_call(
        functools.partial(_inproj_raster_kernel, tn=tn),
        grid=(b, l // tm, n // tn),
        in_specs=[pl.BlockSpec((1, tm, d), lambda i, m, j: (i, m, 0)),
                  pl.BlockSpec((1, d), lambda i, m, j: (0, 0)),
                  pl.BlockSpec((1, 1, d), lambda i, m, j: (i, 0, 0)),
                  pl.BlockSpec((1, 1, d), lambda i, m, j: (i, 0, 0)),
                  pl.BlockSpec((d, tn), lambda i, m, j: (0, j)),
                  pl.BlockSpec((d, LANES), lambda i, m, j: (0, 0))],
        out_specs=[pl.BlockSpec((1, tn // LANES, tm, LANES), lambda i, m, j: (i, j, m, 0)),
                   pl.BlockSpec((1, tm, LANES), lambda i, m, j: (i, m, 0))],
        out_shape=[jax.ShapeDtypeStruct((b, n // LANES, l, LANES), BF16),
                   jax.ShapeDtypeStruct((b, l, LANES), F32)],
        scratch_shapes=[pltpu.VMEM((tm, d), BF16)],
        compiler_params=_cparams(("parallel", "parallel", "arbitrary")),
        name="inproj_raster",
    )(x, nw, sc, sh, w, wgb)


def _inproj_tb_kernel(x_ref, nw_ref, sc_ref, sh_ref, w_ref, o_ref, h_scr, *, r, wb, d, nb):
    per_b = r * wb
    for i in range(nb):
        a = nw_ref[...] * (1.0 + sc_ref[i:i + 1, :])
        s = sh_ref[i:i + 1, :]
        for w in range(wb):
            h = _rms_rows(x_ref[i, :, w * d:(w + 1) * d]) * a + s
            h_scr[(i * wb + w) * r:(i * wb + w + 1) * r, :] = h.astype(BF16)
    acc = _dot(h_scr[...], w_ref[...])
    for j in range(w_ref.shape[1] // LANES):
        for i in range(nb):
            o_ref[j, pl.ds(i, per_b, stride=nb), :] = acc[i * per_b:(i + 1) * per_b, j * LANES:(j + 1) * LANES]


def _inproj_tb(xv, nw, sc, sh, w, r, wb):
    nb, r_total, wd = xv.shape
    d = w.shape[0]
    n = w.shape[1]
    assert r_total % r == 0 and (wd // d) % wb == 0
    if r_total != r:
        assert wb == 1 and wd == d
        grid = (r_total // r,)
        x_spec = pl.BlockSpec((nb, r, d), lambda m: (0, m, 0))
    else:
        grid = (wd // (wb * d),)
        x_spec = pl.BlockSpec((nb, r, wb * d), lambda m: (0, 0, m))
    rows_blk = r * wb * nb
    total_rows = r_total * (wd // d) * nb
    return pl.pallas_call(
        functools.partial(_inproj_tb_kernel, r=r, wb=wb, d=d, nb=nb),
        grid=grid,
        in_specs=[x_spec,
                  pl.BlockSpec((1, d), lambda m: (0, 0)),
                  pl.BlockSpec((nb, d), lambda m: (0, 0)),
                  pl.BlockSpec((nb, d), lambda m: (0, 0)),
                  pl.BlockSpec((d, n), lambda m: (0, 0))],
        out_specs=pl.BlockSpec((n // LANES, rows_blk, LANES), lambda m: (0, m, 0)),
        out_shape=jax.ShapeDtypeStruct((n // LANES, total_rows, LANES), F32),
        scratch_shapes=[pltpu.VMEM((rows_blk, d), BF16)],
        compiler_params=_cparams(("parallel",)),
        name="inproj_tb",
    )(xv, nw, sc, sh, w)


def _lru_kernel(xc_ref, xl_ref, cw_ref, cb_ref, wg_ref, gbias_ref, lam_ref, o_ref,
                hsum, af, uf, ab, ub, *, lc, s, rows, tseg, nb):
    hd = LANES
    pitch = rows * nb + SUBLANES
    cols_seg = tseg // rows
    cw = cw_ref[...]
    cb = cb_ref[...]
    half_nsp = (-0.5 * LRU_C) * _softplus(-lam_ref[...])

    def seg_xr(ref, t0, lseq):
        main = ref[pl.ds(pl.multiple_of(t0 * nb, SUBLANES), tseg * nb), :]
        p0 = jnp.maximum(t0 * nb - 2 * nb, 0)
        prev = ref[pl.ds(pl.multiple_of(p0, SUBLANES), 2 * nb), :]
        prev = jnp.where(t0 > 0, prev, 0.0)
        n0 = jnp.minimum((t0 + tseg) * nb, lseq * nb - nb)
        nxt = ref[pl.ds(pl.multiple_of(n0, SUBLANES), nb), :]
        nxt = jnp.where(t0 + tseg < lseq, nxt, 0.0)
        xm2 = jnp.concatenate([prev, main[:-2 * nb]], axis=0)
        xm1 = jnp.concatenate([prev[nb:], main[:-nb]], axis=0)
        xp1 = jnp.concatenate([main[nb:], nxt], axis=0)
        return cw[0:1] * xm2 + cw[1:2] * xm1 + cw[2:3] * main + cw[3:4] * xp1 + cb

    def gates(xr, direction, a_ref, u_ref):
        lo = direction * 2 * hd
        th_g = jnp.tanh(_dot(xr.astype(BF16), wg_ref[0, :, lo:lo + 2 * hd]) + gbias_ref[0, :, lo:lo + 2 * hd])
        hnsp = half_nsp[:, direction * hd:(direction + 1) * hd]
        log_a = th_g[:, :hd] * hnsp + hnsp
        ig = 0.5 * th_g[:, hd:] + 0.5
        a = jnp.exp(log_a)
        a_ref[...] = a
        sq = -jnp.tanh(log_a) * (1.0 + a * a)
        root = jnp.where(sq > 0.0, sq * lax.rsqrt(sq), 0.0)
        u_ref[...] = root * (ig * xr)

    def scan_segment(hf, hb, store, fseg, bseg):
        fbase = pl.multiple_of(fseg * cols_seg * pitch, SUBLANES)
        bbase = pl.multiple_of(bseg * cols_seg * pitch, SUBLANES)
        for blk in range(tseg // LRU_SCAN_BLOCK):
            for direction in range(N_DIR):
                a_ref, u_ref = (af, uf) if direction == 0 else (ab, ub)
                h0 = hf if direction == 0 else hb
                base = fbase if direction == 0 else bbase
                pa = pu = None
                for kk in range(LRU_SCAN_BLOCK):
                    step = blk * LRU_SCAN_BLOCK + kk
                    t = step if direction == 0 else tseg - 1 - step
                    a = a_ref[t * nb:(t + 1) * nb, :]
                    u = u_ref[t * nb:(t + 1) * nb, :]
                    pa, pu = (a, u) if kk == 0 else (a * pa, a * pu + u)
                    h = pa * h0 + pu
                    off = (t // rows) * pitch + (t % rows) * nb
                    if store == "set":
                        hsum[pl.ds(base + off, nb), :] = h
                    elif store == "add":
                        hsum[pl.ds(base + off, nb), :] = hsum[pl.ds(base + off, nb), :] + h
                if direction == 0:
                    hf = h
                else:
                    hb = h
        return hf, hb

    h0 = jnp.zeros((nb, hd), F32)

    nseg_c = lc // tseg
    def ctx_body(i, carry):
        gates(seg_xr(xc_ref.at[0], i * tseg, lc), 0, af, uf)
        gates(seg_xr(xc_ref.at[0], (nseg_c - 1 - i) * tseg, lc), 1, ab, ub)
        return scan_segment(carry[0], carry[1], None, 0, 0)
    hf, hb = lax.fori_loop(0, nseg_c, ctx_body, (h0, h0))

    nseg_l = s // tseg
    def lat_body(store):
        def body(i, carry):
            bseg = nseg_l - 1 - i
            gates(seg_xr(xl_ref.at[0], i * tseg, s), 0, af, uf)
            gates(seg_xr(xl_ref.at[0], bseg * tseg, s), 1, ab, ub)
            return scan_segment(carry[0], carry[1], store, i, bseg)
        return body
    hf, hb = lax.fori_loop(0, nseg_l // 2, lat_body("set"), (hf, hb))
    lax.fori_loop(nseg_l // 2, nseg_l, lat_body("add"), (hf, hb))

    def out_body(rr, carry):
        for i in range(nb):
            for wg in range(GRID_W // SUBLANES):
                src = wg * SUBLANES * pitch + rr * nb + i
                o_ref[i, pl.ds(pl.multiple_of(rr * GRID_W + wg * SUBLANES, SUBLANES), SUBLANES), :] = (
                    hsum[pl.ds(src, SUBLANES, stride=pitch), :])
        return carry
    lax.fori_loop(0, rows, out_body, 0)


def _lru(xc, xl, conv_w, conv_b, wg, gbias, lam2, nb, lc, s):
    heads = xl.shape[0]
    rows = s // GRID_W
    tseg = 128
    assert lc % tseg == 0 and s % (2 * tseg) == 0 and tseg % rows == 0 and tseg % LRU_SCAN_BLOCK == 0
    pitch = rows * nb + SUBLANES
    return pl.pallas_call(
        functools.partial(_lru_kernel, lc=lc, s=s, rows=rows, tseg=tseg, nb=nb),
        grid=(heads,),
        in_specs=[pl.BlockSpec((1, lc * nb, LANES), lambda h: (h, 0, 0)),
                  pl.BlockSpec((1, s * nb, LANES), lambda h: (h, 0, 0)),
                  pl.BlockSpec((4, LANES), lambda h: (0, h)),
                  pl.BlockSpec((1, LANES), lambda h: (0, h)),
                  pl.BlockSpec((1, LANES, 4 * LANES), lambda h: (h, 0, 0)),
                  pl.BlockSpec((1, 1, 4 * LANES), lambda h: (h, 0, 0)),
                  pl.BlockSpec((1, 2 * LANES), lambda h: (0, h))],
        out_specs=pl.BlockSpec((nb, s, LANES), lambda h: (0, 0, h)),
        out_shape=jax.ShapeDtypeStruct((nb, s, heads * LANES), F32),
        scratch_shapes=[pltpu.VMEM((GRID_W * pitch, LANES), F32)]
                       + [pltpu.VMEM((tseg * nb, LANES), F32)] * 4,
        compiler_params=_cparams(("parallel",)),
        name="lru",
    )(xc, xl, conv_w, conv_b, wg, gbias, lam2)


def _bmm(a, b):
    return lax.dot_general(a, b, (((2,), (1,)), ((0,), (0,))), preferred_element_type=F32)


def _bmm_nt(a, b):
    return lax.dot_general(a, b, (((2,), (2,)), ((0,), (0,))), preferred_element_type=F32)


def _inv_unit_triangular_x4(lm):
    n, c, w = lm.shape
    nblk = w // c
    ri = lax.broadcasted_iota(jnp.int32, (w, w), 0)
    ci = lax.broadcasted_iota(jnp.int32, (w, w), 1)
    on_diag_block = (ri // c) == (ci // c)
    eye = (lax.broadcasted_iota(jnp.int32, (c, w), 0)
           == lax.broadcasted_iota(jnp.int32, (c, w), 1) % c).astype(F32)

    def block_diag(x):
        return jnp.where(on_diag_block, jnp.concatenate([x] * nblk, axis=1), 0.0).astype(BF16)

    p = eye - lm
    lk = _bmm(lm.astype(BF16), block_diag(lm))
    n_sq = int(math.log2(c)) - 2
    for _ in range(n_sq):
        x = _bmm(jnp.concatenate([p, lk], axis=1).astype(BF16), block_diag(lk))
        p = p + x[:, :c]
        lk = x[:, c:]
    return p + _bmm(p.astype(BF16), block_diag(lk))


def _gdn_kernel(qc_ref, kc_ref, vc_ref, ql_ref, kl_ref, vl_ref, z_ref, g_ref, alog_ref, dtb_ref,
                cwq_ref, cwk_ref, cwv_ref, nw_ref, o_ref,
                qs, ks, vs, qcol, grow, grow2, aq_s, b_s, op_s, st_s, o_s, *, lc, s, hg, nbc):
    dk = LANES
    c = CHUNK
    l_all = lc + s
    ncc = lc // c
    ncl = s // c
    nct = ncc + ncl

    ii = lax.broadcasted_iota(jnp.int32, (c, 2 * c), 0)
    jl = lax.broadcasted_iota(jnp.int32, (c, 2 * c), 1)
    fwd_half = jl < c
    jj = jnp.where(fwd_half, jl, jl - c)
    keep2 = (fwd_half & (ii >= jj)) | ((jl >= c) & (ii <= jj))
    strict2 = (fwd_half & (ii > jj)) | ((jl >= c) & (ii < jj))
    lane1 = lax.broadcasted_iota(jnp.int32, (1, LANES), 1)
    eye_k = lax.broadcasted_iota(jnp.int32, (dk, dk), 0) == lax.broadcasted_iota(jnp.int32, (dk, dk), 1)
    blk = 2 * LANES
    si = lax.broadcasted_iota(jnp.int32, (blk, blk), 0)
    ji = lax.broadcasted_iota(jnp.int32, (blk, blk), 1)
    same = (si // c) == (ji // c)
    t_pre = jnp.where(same & (si <= ji), 1.0, 0.0).astype(F32)
    t_suf = jnp.where(same & (si >= ji), 1.0, 0.0).astype(F32)

    def conv_silu(ref, hd, cw, n):
        x = ref[0, hd].astype(F32)
        e = 2 * SUBLANES

        def taps(xm2, xm1, x0, xp1):
            return cw[0:1] * xm2 + cw[1:2] * xm1 + cw[2:3] * x0 + cw[3:4] * xp1

        body = taps(pltpu.roll(x, 2, 0), pltpu.roll(x, 1, 0), x, pltpu.roll(x, n - 1, 0))
        rid = lax.broadcasted_iota(jnp.int32, (e, dk), 0)
        top = x[0:e]
        top = taps(jnp.where(rid >= 2, pltpu.roll(top, 2, 0), 0.0), jnp.where(rid >= 1, pltpu.roll(top, 1, 0), 0.0),
                   top, pltpu.roll(top, e - 1, 0))[0:SUBLANES]
        bot = x[n - e:n]
        bot = taps(pltpu.roll(bot, 2, 0), pltpu.roll(bot, 1, 0), bot,
                   jnp.where(rid < e - 1, pltpu.roll(bot, e - 1, 0), 0.0))[SUBLANES:e]
        return _silu(jnp.concatenate([top, body[SUBLANES:n - SUBLANES], bot], axis=0))

    def l2n(t):
        return t * lax.rsqrt(jnp.sum(t * t, axis=-1, keepdims=True) + NORM_EPS)

    for hd in range(hg):
        cwq = cwq_ref[:, hd * dk:(hd + 1) * dk]
        cwk = cwk_ref[:, hd * dk:(hd + 1) * dk]
        cwv = cwv_ref[:, hd * dk:(hd + 1) * dk]
        qs[0:lc] = l2n(conv_silu(qc_ref, hd, cwq, lc)) * (dk ** -0.5)
        qs[lc:l_all] = l2n(conv_silu(ql_ref, hd, cwq, s)) * (dk ** -0.5)
        ks[0:lc] = l2n(conv_silu(kc_ref, hd, cwk, lc))
        ks[lc:l_all] = l2n(conv_silu(kl_ref, hd, cwk, s))
        vs[0:lc] = conv_silu(vc_ref, hd, cwv, lc)
        vs[lc:l_all] = conv_silu(vl_ref, hd, cwv, s)

        r = g_ref[0, hd]
        rid8 = lax.broadcasted_iota(jnp.int32, (SUBLANES, l_all), 0)
        aneg = -jnp.exp(alog_ref[hd][:, 0:1])
        dtb = dtb_ref[hd][:, 0:1]
        val = jnp.where(rid8 < 2, _sigmoid(r), aneg * _softplus(r + dtb))
        for i in range(l_all // blk):
            vb = val[:, i * blk:(i + 1) * blk]
            pre = jnp.dot(vb, t_pre, preferred_element_type=F32, precision=lax.Precision.HIGHEST)
            suf = jnp.dot(vb, t_suf, preferred_element_type=F32, precision=lax.Precision.HIGHEST)
            rb = lax.broadcasted_iota(jnp.int32, (SUBLANES, blk), 0)
            gblk = jnp.where(rb == 2, pre, jnp.where(rb == 3, suf, vb))
            grow[:, i * blk:(i + 1) * blk] = gblk
            for p2 in range(blk // LANES):
                g128 = gblk[:, p2 * LANES:(p2 + 1) * LANES]
                rolled = pltpu.roll(g128, c, 1)
                ch = (i * (blk // LANES) + p2) * 2
                grow2[0:1, ch * LANES:(ch + 1) * LANES] = jnp.where(lane1 < c, g128[2:3], rolled[3:4])
                grow2[0:1, (ch + 1) * LANES:(ch + 2) * LANES] = jnp.where(lane1 < c, rolled[2:3], g128[3:4])
        gfull = jnp.concatenate([grow[...], jnp.zeros((LANES - SUBLANES, l_all), F32)], axis=0)
        qcol[...] = gfull.T

        def intra(it, carry, hd=hd):
            r0 = pl.multiple_of(it * (nbc * c), 2 * c)
            q = qs[pl.ds(r0, nbc * c), :].reshape(nbc, c, dk)
            k = ks[pl.ds(r0, nbc * c), :].reshape(nbc, c, dk)
            v = vs[pl.ds(r0, nbc * c), :].reshape(nbc, c, dk)
            col = qcol[pl.ds(r0, nbc * c), :].reshape(nbc, c, LANES)
            g2 = grow2[0:1, pl.ds(pl.multiple_of(it * (nbc * LANES), LANES), nbc * LANES)]
            grw2 = jnp.stack([g2[:, j * LANES:(j + 1) * LANES] for j in range(nbc)], axis=0)
            beta = [col[:, :, d:d + 1] for d in range(N_DIR)]
            gcol = [col[:, :, 2 + d:3 + d] for d in range(N_DIR)]
            beta2 = jnp.where(fwd_half, beta[0], beta[1])
            gcol2 = jnp.where(fwd_half, gcol[0], gcol[1])
            kb16 = k.astype(BF16)
            both = _bmm_nt(jnp.concatenate([q.astype(BF16), kb16], axis=1),
                           jnp.concatenate([kb16, kb16], axis=1))
            decay2 = jnp.exp(jnp.where(keep2, gcol2 - grw2, NEG_BIG))
            lm2 = jnp.where(strict2, beta2 * both[:, c:] * decay2, 0.0)
            qkd2 = (both[:, :c] * decay2).astype(BF16)
            half = nbc // 2
            tinv4 = _inv_unit_triangular_x4(jnp.concatenate([lm2[:half], lm2[half:]], axis=2))
            tinv2 = jnp.concatenate([tinv4[:, :, :2 * c], tinv4[:, :, 2 * c:]], axis=0)
            eg = [jnp.exp(g) for g in gcol]
            glast = [gcol[0][:, c - 1:c, :], gcol[1][:, 0:1, :]]
            rhs = [jnp.concatenate([v * beta[d], (k * beta[d]) * eg[d]], axis=2).astype(BF16) for d in range(N_DIR)]
            zero = jnp.zeros((nbc, c, 2 * dk), BF16)
            rhs_bd = jnp.concatenate([jnp.concatenate([rhs[0], zero], axis=2),
                                      jnp.concatenate([zero, rhs[1]], axis=2)], axis=1)
            uw2 = _bmm(tinv2.astype(BF16), rhs_bd).astype(BF16)
            uw_bd = jnp.concatenate([jnp.concatenate([uw2[:, :, :2 * dk], zero], axis=2),
                                     jnp.concatenate([zero, uw2[:, :, 2 * dk:]], axis=2)], axis=1)
            kd = jnp.concatenate([k * jnp.exp(glast[d] - gcol[d]) for d in range(N_DIR)], axis=1)
            kdt2 = jnp.swapaxes(kd, 1, 2).astype(BF16)
            m2 = _bmm(jnp.concatenate([kdt2, qkd2], axis=1), uw_bd)
            for d in range(N_DIR):
                m = m2[:, :, d * 2 * dk:(d + 1) * 2 * dk]
                a_mat = jnp.where(eye_k, jnp.exp(glast[d]), 0.0) - m[:, :dk, dk:]
                q_mat = q * eg[d] - m[:, dk:, dk:]
                aq_s[hd, d, pl.ds(it * nbc, nbc)] = jnp.concatenate([a_mat, q_mat], axis=1).astype(BF16)
                b_s[hd, d, pl.ds(it * nbc, nbc)] = m[:, :dk, :dk].astype(BF16)
                op_s[hd, d, pl.ds(it * nbc, nbc)] = m[:, dk:, :dk].astype(BF16)
            return carry
        for it in range(nct // nbc):
            intra(it, 0)

    st_s[...] = jnp.zeros_like(st_s)
    o_s[...] = jnp.zeros_like(o_s)

    def recur(t, carry):
        cf = t
        cb = jnp.where(t < ncc, ncc - 1 - t, 2 * ncc + ncl - 1 - t)
        chains = [(hd, d, ci) for hd in range(hg) for d, ci in ((0, cf), (1, cb))]
        loaded = []
        for hd, d, ci in chains:
            ro = pl.multiple_of(jnp.where(t >= ncc, ci * c - lc, s), c)
            loaded.append((aq_s[hd, d, ci], st_s[hd, d], b_s[hd, d, ci], op_s[hd, d, ci],
                           o_s[hd, pl.ds(ro, c), :], ro))
        results = []
        for aq, st, bm, om, o_old, ro in loaded:
            x = _dot(aq, st.astype(BF16))
            results.append((x[:dk] + bm.astype(F32), o_old + x[dk:] + om.astype(F32), ro))
        for (hd, d, ci), (st_new, o_new, ro) in zip(chains, results):
            st_s[hd, d] = st_new
            o_s[hd, pl.ds(ro, c), :] = o_new
        return carry
    lax.fori_loop(0, nct, recur, 0)

    for hd in range(hg):
        z = z_ref[0, hd].astype(F32)
        o_ref[0, :, hd * dk:(hd + 1) * dk] = (
            (_rms_rows(o_s[hd, 0:s]) * nw_ref[...]).astype(F32) * _silu(z)).astype(o_ref.dtype)


def _gdn(pc, plat, grow, alog8, dtb8, conv_w, norm_w, lc, s):
    b = plat.shape[0]
    heads = GDN_HEADS
    l_all = lc + s
    nct = l_all // CHUNK
    assert lc % CHUNK == 0 and s % CHUNK == 0 and l_all % (2 * LANES) == 0
    assert (s // CHUNK) % 2 == 0 and GDN_CHUNK_BATCH % 2 == 0 and nct % GDN_CHUNK_BATCH == 0
    hg = GDN_HEADS_PER_STEP
    assert heads % hg == 0
    def slot(base):
        return lambda i, h: (i, base // hg + h, 0, 0)
    def cw(base):
        return lambda i, h: (0, base // hg + h)
    return pl.pallas_call(
        functools.partial(_gdn_kernel, lc=lc, s=s, hg=hg, nbc=GDN_CHUNK_BATCH),
        grid=(b, heads // hg),
        in_specs=[pl.BlockSpec((1, hg, lc, LANES), slot(0)),
                  pl.BlockSpec((1, hg, lc, LANES), slot(heads)),
                  pl.BlockSpec((1, hg, lc, LANES), slot(2 * heads)),
                  pl.BlockSpec((1, hg, s, LANES), slot(heads)),
                  pl.BlockSpec((1, hg, s, LANES), slot(2 * heads)),
                  pl.BlockSpec((1, hg, s, LANES), slot(3 * heads)),
                  pl.BlockSpec((1, hg, s, LANES), slot(4 * heads)),
                  pl.BlockSpec((1, hg, SUBLANES, l_all), lambda i, h: (i, h, 0, 0)),
                  pl.BlockSpec((hg, SUBLANES, LANES), lambda i, h: (h, 0, 0)),
                  pl.BlockSpec((hg, SUBLANES, LANES), lambda i, h: (h, 0, 0)),
                  pl.BlockSpec((4, hg * LANES), cw(0)),
                  pl.BlockSpec((4, hg * LANES), cw(heads)),
                  pl.BlockSpec((4, hg * LANES), cw(2 * heads)),
                  pl.BlockSpec((1, LANES), lambda i, h: (0, 0))],
        out_specs=pl.BlockSpec((1, s, hg * LANES), lambda i, h: (i, 0, h)),
        out_shape=jax.ShapeDtypeStruct((b, s, heads * LANES), BF16),
        scratch_shapes=[pltpu.VMEM((l_all, LANES), F32),
                        pltpu.VMEM((l_all, LANES), F32),
                        pltpu.VMEM((l_all, LANES), F32),
                        pltpu.VMEM((l_all, LANES), F32),
                        pltpu.VMEM((SUBLANES, l_all), F32),
                        pltpu.VMEM((SUBLANES, nct * LANES), F32),
                        pltpu.VMEM((hg, N_DIR, nct, LANES + CHUNK, LANES), BF16),
                        pltpu.VMEM((hg, N_DIR, nct, LANES, LANES), BF16),
                        pltpu.VMEM((hg, N_DIR, nct, CHUNK, LANES), BF16),
                        pltpu.VMEM((hg, N_DIR, LANES, LANES), F32),
                        pltpu.VMEM((hg, s + CHUNK, LANES), F32)],
        compiler_params=_cparams(("parallel", "arbitrary")),
        name="gdn",
    )(pc, pc, pc, plat, plat, plat, plat, grow, alog8, dtb8, conv_w, conv_w, conv_w, norm_w)


def _out_kernel(hs_ref, y_ref, gdn_ref, w_ref, x_ref, gm_ref, scf_ref, shf_ref, nw1_ref, nw2_ref,
                x1_ref, h2_ref, *, heads):
    half = heads * LANES
    y = jnp.concatenate([y_ref[0, j] for j in range(heads)], axis=-1).astype(F32)
    lru = (hs_ref[0] * _gelu_tanh(y)).astype(BF16)
    m = _dot(lru, w_ref[0:half, :]) + _dot(gdn_ref[0], w_ref[half:, :])
    x1 = x_ref[0] + gm_ref[0] * (_rms_rows(m) * nw1_ref[...])
    x1_ref[0] = x1
    h2_ref[0] = (_rms_rows(x1) * nw2_ref[...] * (1.0 + scf_ref[0]) + shf_ref[0]).astype(BF16)


def _out_proj(hs, plat, gdn, w_out, x, gm, scf, shf, nw1, nw2, tm):
    b, s, d = x.shape
    heads = LRU_HEADS
    dm = w_out.shape[0]
    row = lambda i, m: (i, m, 0)
    vec = lambda i, m: (i, 0, 0)
    fix = lambda i, m: (0, 0)
    return pl.pallas_call(
        functools.partial(_out_kernel, heads=heads),
        grid=(b, s // tm),
        in_specs=[pl.BlockSpec((1, tm, heads * LANES), row),
                  pl.BlockSpec((1, heads, tm, LANES), lambda i, m: (i, 0, m, 0)),
                  pl.BlockSpec((1, tm, dm - heads * LANES), row),
                  pl.BlockSpec((dm, d), fix),
                  pl.BlockSpec((1, tm, d), row),
                  pl.BlockSpec((1, 1, d), vec),
                  pl.BlockSpec((1, 1, d), vec),
                  pl.BlockSpec((1, 1, d), vec),
                  pl.BlockSpec((1, d), fix),
                  pl.BlockSpec((1, d), fix)],
        out_specs=[pl.BlockSpec((1, tm, d), row), pl.BlockSpec((1, tm, d), row)],
        out_shape=[jax.ShapeDtypeStruct((b, s, d), F32), jax.ShapeDtypeStruct((b, s, d), BF16)],
        compiler_params=_cparams(("parallel", "parallel")),
        name="out_proj",
    )(hs, plat, gdn, w_out, x, gm, scf, shf, nw1, nw2)


def _mlp_kernel(h_ref, w1_ref, w2_ref, x1_ref, gf_ref, nw_ref, o_ref):
    f = pl.program_id(2)

    @pl.when(f == 0)
    def _():
        o_ref[0] = jnp.zeros_like(o_ref[0])

    hid = jnp.maximum(_dot(h_ref[0], w1_ref[...]), 0.0)
    o_ref[0] = o_ref[0] + _dot((hid * hid).astype(BF16), w2_ref[...])

    @pl.when(f == pl.num_programs(2) - 1)
    def _():
        o_ref[0] = x1_ref[0] + gf_ref[0] * (_rms_rows(o_ref[0]) * nw_ref[...])


def _mlp(h2, w1, w2, x1, gf, nw3, tm, tf):
    b, s, d = x1.shape
    ff = w1.shape[1]
    row = lambda i, m, f: (i, m, 0)
    return pl.pallas_call(
        _mlp_kernel,
        grid=(b, s // tm, ff // tf),
        in_specs=[pl.BlockSpec((1, tm, d), row),
                  pl.BlockSpec((d, tf), lambda i, m, f: (0, f)),
                  pl.BlockSpec((tf, d), lambda i, m, f: (f, 0)),
                  pl.BlockSpec((1, tm, d), row),
                  pl.BlockSpec((1, 1, d), lambda i, m, f: (i, 0, 0)),
                  pl.BlockSpec((1, d), lambda i, m, f: (0, 0))],
        out_specs=pl.BlockSpec((1, tm, d), row),
        out_shape=jax.ShapeDtypeStruct((b, s, d), F32),
        compiler_params=_cparams(("parallel", "parallel", "arbitrary")),
        name="mlp",
    )(h2, w1, w2, x1, gf, nw3)


def kernel(x, c, ctx, c_ctx, w_mod, b_mod, norm_w, w_in, lru_conv_w, lru_conv_b, lru_gate_w, lru_gate_b,
           lru_lambda, gdn_conv_w, gdn_a_log, gdn_dt_bias, gdn_norm_w, w_out, w_ff1, w_ff2):
    b, s, d = x.shape
    lc = ctx.shape[1]
    l_all = lc + s
    rows = s // GRID_W
    d_lru = LRU_HEADS * LANES
    d_gdn = GDN_HEADS * LANES
    assert w_mod.shape[0] == 1 and b == SUBLANES and d == d_lru + d_gdn

    cc = jnp.concatenate([c, c_ctx[None], jnp.zeros((2 * SUBLANES - b - 1, d), F32)], axis=0)
    mod = _modulation(cc, w_mod[0], b_mod[0])
    sh_m, sc_m, g_m, sh_f, sc_f, g_f = [mod[:b, i * d:(i + 1) * d] for i in range(6)]
    csh_m = jnp.broadcast_to(mod[b:b + 1, 0:d], (b, d))
    csc_m = jnp.broadcast_to(mod[b:b + 1, d:2 * d], (b, d))
    nw = norm_w[0]
    nw0, nw1, nw2, nw3 = [nw[i:i + 1] for i in range(4)]
    v3 = lambda t: t.reshape(b, 1, d)

    w_in0 = w_in[0]
    y_end = 2 * d_lru
    qkv_end = y_end + 3 * d_gdn
    z_end = qkv_end + d_gdn
    w_x = w_in0[:, :d_lru].astype(BF16)
    w_lat = w_in0[:, d_lru:z_end].astype(BF16)
    w_ctx = w_in0[:, y_end:qkv_end].astype(BF16)
    n_gb = w_in0.shape[1] - z_end
    w_gb = jnp.pad(w_in0[:, z_end:], ((0, 0), (0, LANES - n_gb))).astype(BF16)

    p_lat, gb_lat = _inproj_raster(x, nw0, v3(sc_m), v3(sh_m), w_lat, w_gb, tm=1024, tn=512)
    p_ctx, gb_ctx = _inproj_raster(ctx, nw0, v3(csc_m), v3(csh_m), w_ctx, w_gb, tm=lc, tn=512)
    xl = _inproj_tb(x.reshape(b, rows, GRID_W * d), nw0, sc_m, sh_m, w_x, r=rows, wb=4)
    xc = _inproj_tb(ctx, nw0, csc_m, csh_m, w_x, r=128, wb=1)

    gw = lru_gate_w[0]
    wg = (0.5 * jnp.transpose(gw, (2, 3, 0, 1, 4))).reshape(LRU_HEADS, LANES, 4 * LANES).astype(BF16)
    gbias = 0.5 * jnp.transpose(lru_gate_b[0].reshape(N_DIR, 2, LRU_HEADS, LANES), (2, 0, 1, 3))
    gbias = gbias.reshape(LRU_HEADS, 1, 4 * LANES)
    lam2 = jnp.transpose(lru_lambda[0].reshape(N_DIR, LRU_HEADS, LANES), (1, 0, 2)).reshape(1, 2 * d_lru)
    hs = _lru(xc, xl, lru_conv_w[0], lru_conv_b[0].reshape(1, d_lru), wg, gbias, lam2, b, lc, s)

    gb = jnp.concatenate([gb_ctx, gb_lat], axis=1)[:, :, :n_gb]
    gb = gb.reshape(b, l_all, 2, N_DIR, GDN_HEADS)
    grow = jnp.transpose(gb, (0, 4, 2, 3, 1)).reshape(b, GDN_HEADS, 2 * N_DIR, l_all)
    grow = jnp.pad(grow, ((0, 0), (0, 0), (0, SUBLANES - 2 * N_DIR), (0, 0)))
    def rows8(t):
        t = jnp.transpose(t, (1, 0))[:, :, None]
        t = jnp.pad(t, ((0, 0), (2, SUBLANES - 2 - N_DIR), (0, 0)))
        return jnp.broadcast_to(t, (GDN_HEADS, SUBLANES, LANES)).astype(F32)
    gdn = _gdn(p_ctx, p_lat, grow, rows8(gdn_a_log[0]), rows8(gdn_dt_bias[0]), gdn_conv_w[0],
               gdn_norm_w[0].reshape(1, LANES), lc, s)

    x1, h2 = _out_proj(hs, p_lat, gdn, w_out[0].astype(BF16), x, v3(g_m), v3(sc_f), v3(sh_f),
                       nw1, nw2, tm=512)
    return _mlp(h2, w_ff1[0].astype(BF16), w_ff2[0].astype(BF16), x1, v3(g_f), nw3, tm=512, tf=1024)
```

```python
import functools
import math

import jax
import jax.numpy as jnp
from jax import lax
from jax.experimental import pallas as pl
from jax.experimental.pallas import tpu as pltpu

F32 = jnp.float32
BF16 = jnp.bfloat16

LANES = 128
SUBLANES = 8
VMEM_LIMIT = 56 * 1024 * 1024

GRID_W = 64
NORM_EPS = 1e-6
LRU_C = 8.0
LRU_HEADS = 8
GDN_HEADS = 8
N_DIR = 2
CHUNK = 64
NEG_BIG = -1e30
OUT_SUB_ROWS = 256
LRU_SCAN_BLOCK = 8
GDN_HEADS_PER_STEP = 2
GDN_CHUNK_BATCH = 18


def _cparams(sem):
    return pltpu.CompilerParams(dimension_semantics=sem, vmem_limit_bytes=VMEM_LIMIT)


def _dot(a, b):
    return jnp.dot(a, b, preferred_element_type=F32)


def _dot_nt(a, b):
    return lax.dot_general(a, b, (((1,), (1,)), ((), ())), preferred_element_type=F32)


def _sigmoid(t):
    return 0.5 * jnp.tanh(0.5 * t) + 0.5


def _softplus(t):
    return jnp.maximum(t, 0.0) + jnp.log(1.0 + jnp.exp(-jnp.abs(t)))


def _silu(t):
    return t * _sigmoid(t)


def _gelu_tanh(t):
    return 0.5 * t * (1.0 + jnp.tanh(math.sqrt(2.0 / math.pi) * (t + 0.044715 * (t * t * t))))


def _rms_rows(t):
    return t * lax.rsqrt(jnp.mean(t * t, axis=-1, keepdims=True) + NORM_EPS)


def _mod_kernel(c_ref, w_ref, b_ref, o_ref):
    s = _silu(c_ref[...])
    o_ref[...] = _dot(s.astype(BF16), w_ref[...].astype(BF16)) + b_ref[...]


def _modulation(cc, w_mod, b_mod):
    rows, d = cc.shape
    n = w_mod.shape[1]
    tn = 1024
    return pl.pallas_call(
        _mod_kernel,
        grid=(n // tn,),
        in_specs=[pl.BlockSpec((rows, d), lambda j: (0, 0)),
                  pl.BlockSpec((d, tn), lambda j: (0, j)),
                  pl.BlockSpec((1, tn), lambda j: (0, j))],
        out_specs=pl.BlockSpec((rows, tn), lambda j: (0, j)),
        out_shape=jax.ShapeDtypeStruct((rows, n), F32),
        compiler_params=_cparams(("arbitrary",)),
        name="mod",
    )(cc, w_mod, b_mod.reshape(1, n))


def _inproj_raster_kernel(x_ref, nw_ref, sc_ref, sh_ref, w_ref, wgb_ref, p_ref, gb_ref, h_scr, *, tn):
    @pl.when(pl.program_id(2) == 0)
    def _():
        a = nw_ref[...] * (1.0 + sc_ref[0])
        h = (_rms_rows(x_ref[0]) * a + sh_ref[0]).astype(BF16)
        h_scr[...] = h
        gb_ref[0] = _dot(h, wgb_ref[...])

    acc = _dot(h_scr[...], w_ref[...])
    for j in range(tn // LANES):
        p_ref[0, j] = acc[:, j * LANES:(j + 1) * LANES].astype(p_ref.dtype)


def _inproj_raster(x, nw, sc, sh, w, wgb, col0, n, tm, tn):
    b, l, d = x.shape
    assert col0 % tn == 0 and n % tn == 0
    off = col0 // tn
    return pl.pallas_call(
        functools.partial(_inproj_raster_kernel, tn=tn),
        grid=(b, l // tm, n // tn),
        in_specs=[pl.BlockSpec((1, tm, d), lambda i, m, j: (i, m, 0)),
                  pl.BlockSpec((1, d), lambda i, m, j: (0, 0)),
                  pl.BlockSpec((1, 1, d), lambda i, m, j: (i, 0, 0)),
                  pl.BlockSpec((1, 1, d), lambda i, m, j: (i, 0, 0)),
                  pl.BlockSpec((d, tn), lambda i, m, j: (0, j + off)),
                  pl.BlockSpec((d, LANES), lambda i, m, j: (0, 0))],
        out_specs=[pl.BlockSpec((1, tn // LANES, tm, LANES), lambda i, m, j: (i, j, m, 0)),
                   pl.BlockSpec((1, tm, LANES), lambda i, m, j: (i, m, 0))],
        out_shape=[jax.ShapeDtypeStruct((b, n // LANES, l, LANES), BF16),
                   jax.ShapeDtypeStruct((b, l, LANES), F32)],
        scratch_shapes=[pltpu.VMEM((tm, d), BF16)],
        compiler_params=_cparams(("parallel", "parallel", "arbitrary")),
        name="inproj_raster",
    )(x, nw, sc, sh, w, wgb)


def _inproj_tb_kernel(x_ref, nw_ref, sc_ref, sh_ref, w_ref, o_ref, h_scr, *, nbh, nb):
    bh = pl.program_id(1)
    per_b = x_ref.shape[1] * x_ref.shape[2]
    d = x_ref.shape[3]
    for i in range(nbh):
        a = nw_ref[...] * (1.0 + sc_ref[i])
        h = _rms_rows(x_ref[i].reshape(per_b, d)) * a + sh_ref[i]
        h_scr[i * per_b:(i + 1) * per_b, :] = h.astype(BF16)
    acc = _dot(h_scr[...], w_ref[...])
    for j in range(w_ref.shape[1] // LANES):
        for i in range(nbh):
            o_ref[j, pl.ds(bh * nbh + i, per_b, stride=nb), :] = (
                acc[i * per_b:(i + 1) * per_b, j * LANES:(j + 1) * LANES])


def _inproj_tb(xv, nw, sc, sh, w, n, r, nbh):
    nb, r_total, c_total, d = xv.shape
    cw = SUBLANES
    assert nb % nbh == 0 and r_total % r == 0 and c_total % cw == 0 and (c_total == cw or r_total == r)
    per_b = r * cw
    n_m = (r_total // r) * (c_total // cw)
    x_map = (lambda m, bh: (bh, m, 0, 0)) if c_total == cw else (lambda m, bh: (bh, 0, m, 0))
    return pl.pallas_call(
        functools.partial(_inproj_tb_kernel, nbh=nbh, nb=nb),
        grid=(n_m, nb // nbh),
        in_specs=[pl.BlockSpec((nbh, r, cw, d), x_map),
                  pl.BlockSpec((1, d), lambda m, bh: (0, 0)),
                  pl.BlockSpec((nbh, 1, d), lambda m, bh: (bh, 0, 0)),
                  pl.BlockSpec((nbh, 1, d), lambda m, bh: (bh, 0, 0)),
                  pl.BlockSpec((d, n), lambda m, bh: (0, 0))],
        out_specs=pl.BlockSpec((n // LANES, per_b * nb, LANES), lambda m, bh: (0, m, 0)),
        out_shape=jax.ShapeDtypeStruct((n // LANES, n_m * per_b * nb, LANES), F32),
        scratch_shapes=[pltpu.VMEM((nbh * per_b, d), BF16)],
        compiler_params=_cparams(("parallel", "arbitrary")),
        name="inproj_tb",
    )(xv, nw, sc, sh, w)


def _lru_kernel(xc_ref, xl_ref, cw_ref, cb_ref, wg_ref, gbias_ref, lam_ref, o_ref,
                hsum, af, uf, ab, ub, *, lc, s, rows, nb):
    hd = LANES
    cpb = SUBLANES
    tseg = cpb * rows
    nblk = GRID_W // cpb
    cw = cw_ref[...]
    cb = cb_ref[...]
    half_nsp = (-0.5 * LRU_C) * _softplus(-lam_ref[...])

    def conv(prev, main, nxt):
        xm2 = jnp.concatenate([prev, main[:-2 * nb]], axis=0)
        xm1 = jnp.concatenate([prev[nb:], main[:-nb]], axis=0)
        xp1 = jnp.concatenate([main[nb:], nxt], axis=0)
        return cw[0:1] * xm2 + cw[1:2] * xm1 + cw[2:3] * main + cw[3:4] * xp1 + cb

    def ctx_xr(t0):
        ref = xc_ref.at[0]
        main = ref[pl.ds(pl.multiple_of(t0 * nb, SUBLANES), tseg * nb), :]
        p0 = jnp.maximum(t0 * nb - 2 * nb, 0)
        prev = ref[pl.ds(pl.multiple_of(p0, SUBLANES), 2 * nb), :]
        n0 = jnp.minimum((t0 + tseg) * nb, lc * nb - nb)
        nxt = ref[pl.ds(pl.multiple_of(n0, SUBLANES), nb), :]
        return conv(jnp.where(t0 > 0, prev, 0.0), main, jnp.where(t0 + tseg < lc, nxt, 0.0))

    def lat_xr(sg):
        main = jnp.concatenate([xl_ref[0, sg, :, w * nb:(w + 1) * nb, :].reshape(rows * nb, hd)
                                for w in range(cpb)], axis=0)
        prev = xl_ref[0, jnp.maximum(sg - 1, 0), rows - 2:rows, (cpb - 1) * nb:cpb * nb, :].reshape(2 * nb, hd)
        nxt = xl_ref[0, jnp.minimum(sg + 1, nblk - 1), 0:1, 0:nb, :].reshape(nb, hd)
        return conv(jnp.where(sg > 0, prev, 0.0), main, jnp.where(sg < nblk - 1, nxt, 0.0))

    def gates(xr, direction, a_ref, u_ref):
        lo = direction * 2 * hd
        th_g = jnp.tanh(_dot(xr.astype(BF16), wg_ref[0, :, lo:lo + 2 * hd]) + gbias_ref[0, :, lo:lo + 2 * hd])
        hnsp = half_nsp[:, direction * hd:(direction + 1) * hd]
        log_a = th_g[:, :hd] * hnsp + hnsp
        ig = 0.5 * th_g[:, hd:] + 0.5
        a = jnp.exp(log_a)
        a_ref[...] = a
        sq = -jnp.tanh(log_a) * (1.0 + a * a)
        root = jnp.where(sq > 0.0, sq * lax.rsqrt(sq), 0.0)
        u_ref[...] = root * (ig * xr)

    def scan_segment(hf, hb, store, fseg, bseg):
        for blk in range(tseg // LRU_SCAN_BLOCK):
            for direction in range(N_DIR):
                a_ref, u_ref = (af, uf) if direction == 0 else (ab, ub)
                h0 = hf if direction == 0 else hb
                sg = fseg if direction == 0 else bseg
                pa = pu = None
                for kk in range(LRU_SCAN_BLOCK):
                    step = blk * LRU_SCAN_BLOCK + kk
                    t = step if direction == 0 else tseg - 1 - step
                    a = a_ref[t * nb:(t + 1) * nb, :]
                    u = u_ref[t * nb:(t + 1) * nb, :]
                    pa, pu = (a, u) if kk == 0 else (a * pa, a * pu + u)
                    h = pa * h0 + pu
                    w, r = t // rows, t % rows
                    if store == "set":
                        hsum[sg, r, w * nb:(w + 1) * nb, :] = h
                    elif store == "add":
                        hsum[sg, r, w * nb:(w + 1) * nb, :] = hsum[sg, r, w * nb:(w + 1) * nb, :] + h
                if direction == 0:
                    hf = h
                else:
                    hb = h
        return hf, hb

    h0 = jnp.zeros((nb, hd), F32)

    nseg_c = lc // tseg
    def ctx_body(i, carry):
        gates(ctx_xr(i * tseg), 0, af, uf)
        gates(ctx_xr((nseg_c - 1 - i) * tseg), 1, ab, ub)
        return scan_segment(carry[0], carry[1], None, 0, 0)
    hf, hb = lax.fori_loop(0, nseg_c, ctx_body, (h0, h0))

    def lat_body(store):
        def body(i, carry):
            bseg = nblk - 1 - i
            gates(lat_xr(i), 0, af, uf)
            gates(lat_xr(bseg), 1, ab, ub)
            return scan_segment(carry[0], carry[1], store, i, bseg)
        return body
    hf, hb = lax.fori_loop(0, nblk // 2, lat_body("set"), (hf, hb))
    lax.fori_loop(nblk // 2, nblk, lat_body("add"), (hf, hb))

    def out_body(rr, carry):
        for i in range(nb):
            for wg in range(nblk):
                o_ref[i, pl.ds(pl.multiple_of(rr * GRID_W + wg * cpb, SUBLANES), cpb), :] = (
                    hsum[wg, rr, pl.ds(i, cpb, stride=nb), :])
        return carry
    lax.fori_loop(0, rows, out_body, 0)


def _lru(xc, xl, conv_w, conv_b, wg, gbias, lam2, nb, lc, s):
    heads = xl.shape[0]
    rows = s // GRID_W
    tseg = SUBLANES * rows
    nblk = GRID_W // SUBLANES
    assert lc % tseg == 0 and nblk % 2 == 0 and tseg % LRU_SCAN_BLOCK == 0 and nb == SUBLANES
    return pl.pallas_call(
        functools.partial(_lru_kernel, lc=lc, s=s, rows=rows, nb=nb),
        grid=(heads,),
        in_specs=[pl.BlockSpec((1, lc * nb, LANES), lambda h: (h, 0, 0)),
                  pl.BlockSpec((1, nblk, rows, SUBLANES * nb, LANES), lambda h: (h, 0, 0, 0, 0)),
                  pl.BlockSpec((4, LANES), lambda h: (0, h)),
                  pl.BlockSpec((1, LANES), lambda h: (0, h)),
                  pl.BlockSpec((1, LANES, 4 * LANES), lambda h: (h, 0, 0)),
                  pl.BlockSpec((1, 1, 4 * LANES), lambda h: (h, 0, 0)),
                  pl.BlockSpec((1, 2 * LANES), lambda h: (0, h))],
        out_specs=pl.BlockSpec((nb, s, LANES), lambda h: (0, 0, h)),
        out_shape=jax.ShapeDtypeStruct((nb, s, heads * LANES), F32),
        scratch_shapes=[pltpu.VMEM((nblk, rows, SUBLANES * nb, LANES), F32)]
                       + [pltpu.VMEM((tseg * nb, LANES), F32)] * 4,
        compiler_params=_cparams(("parallel",)),
        name="lru",
    )(xc, xl, conv_w, conv_b, wg, gbias, lam2)


def _bmm(a, b):
    return lax.dot_general(a, b, (((2,), (1,)), ((0,), (0,))), preferred_element_type=F32)


def _bmm_nt(a, b):
    return lax.dot_general(a, b, (((2,), (2,)), ((0,), (0,))), preferred_element_type=F32)


def _inv_unit_triangular_x4(lm):
    n, c, w = lm.shape
    nblk = w // c
    ri = lax.broadcasted_iota(jnp.int32, (w, w), 0)
    ci = lax.broadcasted_iota(jnp.int32, (w, w), 1)
    on_diag_block = (ri // c) == (ci // c)
    eye = (lax.broadcasted_iota(jnp.int32, (c, w), 0)
           == lax.broadcasted_iota(jnp.int32, (c, w), 1) % c).astype(F32)

    def block_diag(x):
        return jnp.where(on_diag_block, jnp.concatenate([x] * nblk, axis=1), 0.0).astype(BF16)

    p = eye - lm
    lk = _bmm(lm.astype(BF16), block_diag(lm))
    n_sq = int(math.log2(c)) - 2
    for _ in range(n_sq):
        x = _bmm(jnp.concatenate([p, lk], axis=1).astype(BF16), block_diag(lk))
        p = p + x[:, :c]
        lk = x[:, c:]
    return p + _bmm(p.astype(BF16), block_diag(lk))


def _gdn_kernel(qc_ref, kc_ref, vc_ref, ql_ref, kl_ref, vl_ref, z_ref, g_ref, alog_ref, dtb_ref,
                cwq_ref, cwk_ref, cwv_ref, nw_ref, o_ref,
                qs, ks, vs, qcol, grow, grow2, aq_s, b_s, op_s, st_s, o_s, *, lc, s, hg, nbc):
    dk = LANES
    c = CHUNK
    l_all = lc + s
    ncc = lc // c
    ncl = s // c
    nct = ncc + ncl

    ii = lax.broadcasted_iota(jnp.int32, (c, 2 * c), 0)
    jl = lax.broadcasted_iota(jnp.int32, (c, 2 * c), 1)
    fwd_half = jl < c
    jj = jnp.where(fwd_half, jl, jl - c)
    keep2 = (fwd_half & (ii >= jj)) | ((jl >= c) & (ii <= jj))
    strict2 = (fwd_half & (ii > jj)) | ((jl >= c) & (ii < jj))
    lane1 = lax.broadcasted_iota(jnp.int32, (1, LANES), 1)
    eye_k = lax.broadcasted_iota(jnp.int32, (dk, dk), 0) == lax.broadcasted_iota(jnp.int32, (dk, dk), 1)
    blk = 2 * LANES
    si = lax.broadcasted_iota(jnp.int32, (blk, blk), 0)
    ji = lax.broadcasted_iota(jnp.int32, (blk, blk), 1)
    same = (si // c) == (ji // c)
    t_pre = jnp.where(same & (si <= ji), 1.0, 0.0).astype(F32)
    t_suf = jnp.where(same & (si >= ji), 1.0, 0.0).astype(F32)

    def conv_silu(ref, hd, cw, n):
        x = ref[0, hd].astype(F32)
        e = 2 * SUBLANES

        def taps(xm2, xm1, x0, xp1):
            return cw[0:1] * xm2 + cw[1:2] * xm1 + cw[2:3] * x0 + cw[3:4] * xp1

        body = taps(pltpu.roll(x, 2, 0), pltpu.roll(x, 1, 0), x, pltpu.roll(x, n - 1, 0))
        rid = lax.broadcasted_iota(jnp.int32, (e, dk), 0)
        top = x[0:e]
        top = taps(jnp.where(rid >= 2, pltpu.roll(top, 2, 0), 0.0), jnp.where(rid >= 1, pltpu.roll(top, 1, 0), 0.0),
                   top, pltpu.roll(top, e - 1, 0))[0:SUBLANES]
        bot = x[n - e:n]
        bot = taps(pltpu.roll(bot, 2, 0), pltpu.roll(bot, 1, 0), bot,
                   jnp.where(rid < e - 1, pltpu.roll(bot, e - 1, 0), 0.0))[SUBLANES:e]
        return _silu(jnp.concatenate([top, body[SUBLANES:n - SUBLANES], bot], axis=0))

    def l2n(t):
        return t * lax.rsqrt(jnp.sum(t * t, axis=-1, keepdims=True) + NORM_EPS)

    for hd in range(hg):
        cwq = cwq_ref[:, hd * dk:(hd + 1) * dk]
        cwk = cwk_ref[:, hd * dk:(hd + 1) * dk]
        cwv = cwv_ref[:, hd * dk:(hd + 1) * dk]
        qs[0:lc] = l2n(conv_silu(qc_ref, hd, cwq, lc)) * (dk ** -0.5)
        qs[lc:l_all] = l2n(conv_silu(ql_ref, hd, cwq, s)) * (dk ** -0.5)
        ks[0:lc] = l2n(conv_silu(kc_ref, hd, cwk, lc))
        ks[lc:l_all] = l2n(conv_silu(kl_ref, hd, cwk, s))
        vs[0:lc] = conv_silu(vc_ref, hd, cwv, lc)
        vs[lc:l_all] = conv_silu(vl_ref, hd, cwv, s)

        r = g_ref[0, hd]
        rid8 = lax.broadcasted_iota(jnp.int32, (SUBLANES, l_all), 0)
        aneg = -jnp.exp(alog_ref[hd][:, 0:1])
        dtb = dtb_ref[hd][:, 0:1]
        val = jnp.where(rid8 < 2, _sigmoid(r), aneg * _softplus(r + dtb))
        for i in range(l_all // blk):
            vb = val[:, i * blk:(i + 1) * blk]
            pre = jnp.dot(vb, t_pre, preferred_element_type=F32, precision=lax.Precision.HIGHEST)
            suf = jnp.dot(vb, t_suf, preferred_element_type=F32, precision=lax.Precision.HIGHEST)
            rb = lax.broadcasted_iota(jnp.int32, (SUBLANES, blk), 0)
            gblk = jnp.where(rb == 2, pre, jnp.where(rb == 3, suf, vb))
            grow[:, i * blk:(i + 1) * blk] = gblk
            for p2 in range(blk // LANES):
                g128 = gblk[:, p2 * LANES:(p2 + 1) * LANES]
                rolled = pltpu.roll(g128, c, 1)
                ch = (i * (blk // LANES) + p2) * 2
                grow2[0:1, ch * LANES:(ch + 1) * LANES] = jnp.where(lane1 < c, g128[2:3], rolled[3:4])
                grow2[0:1, (ch + 1) * LANES:(ch + 2) * LANES] = jnp.where(lane1 < c, rolled[2:3], g128[3:4])
        gfull = jnp.concatenate([grow[...], jnp.zeros((LANES - SUBLANES, l_all), F32)], axis=0)
        qcol[...] = gfull.T

        def intra(it, carry, hd=hd):
            r0 = pl.multiple_of(it * (nbc * c), 2 * c)
            q = qs[pl.ds(r0, nbc * c), :].reshape(nbc, c, dk)
            k = ks[pl.ds(r0, nbc * c), :].reshape(nbc, c, dk)
            v = vs[pl.ds(r0, nbc * c), :].reshape(nbc, c, dk)
            col = qcol[pl.ds(r0, nbc * c), :].reshape(nbc, c, LANES)
            g2 = grow2[0:1, pl.ds(pl.multiple_of(it * (nbc * LANES), LANES), nbc * LANES)]
            grw2 = jnp.stack([g2[:, j * LANES:(j + 1) * LANES] for j in range(nbc)], axis=0)
            beta = [col[:, :, d:d + 1] for d in range(N_DIR)]
            gcol = [col[:, :, 2 + d:3 + d] for d in range(N_DIR)]
            beta2 = jnp.where(fwd_half, beta[0], beta[1])
            gcol2 = jnp.where(fwd_half, gcol[0], gcol[1])
            kb16 = k.astype(BF16)
            both = _bmm_nt(jnp.concatenate([q.astype(BF16), kb16], axis=1),
                           jnp.concatenate([kb16, kb16], axis=1))
            decay2 = jnp.exp(jnp.where(keep2, gcol2 - grw2, NEG_BIG))
            lm2 = jnp.where(strict2, beta2 * both[:, c:] * decay2, 0.0)
            qkd2 = (both[:, :c] * decay2).astype(BF16)
            half = nbc // 2
            tinv4 = _inv_unit_triangular_x4(jnp.concatenate([lm2[:half], lm2[half:]], axis=2))
            tinv2 = jnp.concatenate([tinv4[:, :, :2 * c], tinv4[:, :, 2 * c:]], axis=0)
            eg = [jnp.exp(g) for g in gcol]
            glast = [gcol[0][:, c - 1:c, :], gcol[1][:, 0:1, :]]
            rhs = [jnp.concatenate([v * beta[d], (k * beta[d]) * eg[d]], axis=2).astype(BF16) for d in range(N_DIR)]
            zero = jnp.zeros((nbc, c, 2 * dk), BF16)
            rhs_bd = jnp.concatenate([jnp.concatenate([rhs[0], zero], axis=2),
                                      jnp.concatenate([zero, rhs[1]], axis=2)], axis=1)
            uw2 = _bmm(tinv2.astype(BF16), rhs_bd).astype(BF16)
            uw_bd = jnp.concatenate([jnp.concatenate([uw2[:, :, :2 * dk], zero], axis=2),
                                     jnp.concatenate([zero, uw2[:, :, 2 * dk:]], axis=2)], axis=1)
            kd = jnp.concatenate([k * jnp.exp(glast[d] - gcol[d]) for d in range(N_DIR)], axis=1)
            kdt2 = jnp.swapaxes(kd, 1, 2).astype(BF16)
            m2 = _bmm(jnp.concatenate([kdt2, qkd2], axis=1), uw_bd)
            for d in range(N_DIR):
                m = m2[:, :, d * 2 * dk:(d + 1) * 2 * dk]
                a_mat = jnp.where(eye_k, jnp.exp(glast[d]), 0.0) - m[:, :dk, dk:]
                q_mat = q * eg[d] - m[:, dk:, dk:]
                aq_s[hd, d, pl.ds(it * nbc, nbc)] = jnp.concatenate([a_mat, q_mat], axis=1).astype(BF16)
                b_s[hd, d, pl.ds(it * nbc, nbc)] = m[:, :dk, :dk].astype(BF16)
                op_s[hd, d, pl.ds(it * nbc, nbc)] = m[:, dk:, :dk].astype(BF16)
            return carry
        for it in range(nct // nbc):
            intra(it, 0)

    st_s[...] = jnp.zeros_like(st_s)
    o_s[...] = jnp.zeros_like(o_s)

    def recur(t, carry):
        cf = t
        cb = jnp.where(t < ncc, ncc - 1 - t, 2 * ncc + ncl - 1 - t)
        chains = [(hd, d, ci) for hd in range(hg) for d, ci in ((0, cf), (1, cb))]
        loaded = []
        for hd, d, ci in chains:
            ro = pl.multiple_of(jnp.where(t >= ncc, ci * c - lc, s), c)
            loaded.append((aq_s[hd, d, ci], st_s[hd, d], b_s[hd, d, ci], op_s[hd, d, ci],
                           o_s[hd, pl.ds(ro, c), :], ro))
        results = []
        for aq, st, bm, om, o_old, ro in loaded:
            x = _dot(aq, st.astype(BF16))
            results.append((x[:dk] + bm.astype(F32), o_old + x[dk:] + om.astype(F32), ro))
        for (hd, d, ci), (st_new, o_new, ro) in zip(chains, results):
            st_s[hd, d] = st_new
            o_s[hd, pl.ds(ro, c), :] = o_new
        return carry
    lax.fori_loop(0, nct, recur, 0)

    for hd in range(hg):
        z = z_ref[0, hd].astype(F32)
        o_ref[0, :, hd * dk:(hd + 1) * dk] = (
            (_rms_rows(o_s[hd, 0:s]) * nw_ref[...]).astype(F32) * _silu(z)).astype(o_ref.dtype)


def _gdn(pc, plat, grow, alog8, dtb8, conv_w, norm_w, lc, s):
    b = plat.shape[0]
    heads = GDN_HEADS
    l_all = lc + s
    nct = l_all // CHUNK
    assert lc % CHUNK == 0 and s % CHUNK == 0 and l_all % (2 * LANES) == 0
    assert (s // CHUNK) % 2 == 0 and GDN_CHUNK_BATCH % 2 == 0 and nct % GDN_CHUNK_BATCH == 0
    hg = GDN_HEADS_PER_STEP
    assert heads % hg == 0
    def slot(base):
        return lambda i, h: (i, base // hg + h, 0, 0)
    def cw(base):
        return lambda i, h: (0, base // hg + h)
    return pl.pallas_call(
        functools.partial(_gdn_kernel, lc=lc, s=s, hg=hg, nbc=GDN_CHUNK_BATCH),
        grid=(b, heads // hg),
        in_specs=[pl.BlockSpec((1, hg, lc, LANES), slot(0)),
                  pl.BlockSpec((1, hg, lc, LANES), slot(heads)),
                  pl.BlockSpec((1, hg, lc, LANES), slot(2 * heads)),
                  pl.BlockSpec((1, hg, s, LANES), slot(heads)),
                  pl.BlockSpec((1, hg, s, LANES), slot(2 * heads)),
                  pl.BlockSpec((1, hg, s, LANES), slot(3 * heads)),
                  pl.BlockSpec((1, hg, s, LANES), slot(4 * heads)),
                  pl.BlockSpec((1, hg, SUBLANES, l_all), lambda i, h: (i, h, 0, 0)),
                  pl.BlockSpec((hg, SUBLANES, LANES), lambda i, h: (h, 0, 0)),
                  pl.BlockSpec((hg, SUBLANES, LANES), lambda i, h: (h, 0, 0)),
                  pl.BlockSpec((4, hg * LANES), cw(0)),
                  pl.BlockSpec((4, hg * LANES), cw(heads)),
                  pl.BlockSpec((4, hg * LANES), cw(2 * heads)),
                  pl.BlockSpec((1, LANES), lambda i, h: (0, 0))],
        out_specs=pl.BlockSpec((1, s, hg * LANES), lambda i, h: (i, 0, h)),
        out_shape=jax.ShapeDtypeStruct((b, s, heads * LANES), BF16),
        scratch_shapes=[pltpu.VMEM((l_all, LANES), F32),
                        pltpu.VMEM((l_all, LANES), F32),
                        pltpu.VMEM((l_all, LANES), F32),
                        pltpu.VMEM((l_all, LANES), F32),
                        pltpu.VMEM((SUBLANES, l_all), F32),
                        pltpu.VMEM((SUBLANES, nct * LANES), F32),
                        pltpu.VMEM((hg, N_DIR, nct, LANES + CHUNK, LANES), BF16),
                        pltpu.VMEM((hg, N_DIR, nct, LANES, LANES), BF16),
                        pltpu.VMEM((hg, N_DIR, nct, CHUNK, LANES), BF16),
                        pltpu.VMEM((hg, N_DIR, LANES, LANES), F32),
                        pltpu.VMEM((hg, s + CHUNK, LANES), F32)],
        compiler_params=_cparams(("parallel", "arbitrary")),
        name="gdn",
    )(pc, pc, pc, plat, plat, plat, plat, grow, alog8, dtb8, conv_w, conv_w, conv_w, norm_w)


def _out_kernel(hs_ref, y_ref, gdn_ref, w_ref, x_ref, gm_ref, scf_ref, shf_ref, nw1_ref, nw2_ref,
                x1_ref, h2_ref, *, heads):
    half = heads * LANES
    a2 = nw2_ref[...] * (1.0 + scf_ref[0])
    tm = x_ref.shape[1]
    for r0 in range(0, tm, OUT_SUB_ROWS):
        rs = slice(r0, r0 + OUT_SUB_ROWS)
        y = jnp.concatenate([y_ref[0, j, rs, :] for j in range(heads)], axis=-1).astype(F32)
        lru = (hs_ref[0, rs, :] * _gelu_tanh(y)).astype(BF16)
        m = _dot(lru, w_ref[0:half, :]) + _dot(gdn_ref[0, rs, :], w_ref[half:, :])
        x1 = x_ref[0, rs, :] + gm_ref[0] * (_rms_rows(m) * nw1_ref[...])
        x1_ref[0, rs, :] = x1
        h2_ref[0, rs, :] = (_rms_rows(x1) * a2 + shf_ref[0]).astype(BF16)


def _out_proj(hs, plat, gdn, w_out, x, gm, scf, shf, nw1, nw2, tm):
    b, s, d = x.shape
    heads = LRU_HEADS
    dm = w_out.shape[0]
    row = lambda i, m: (i, m, 0)
    vec = lambda i, m: (i, 0, 0)
    fix = lambda i, m: (0, 0)
    return pl.pallas_call(
        functools.partial(_out_kernel, heads=heads),
        grid=(b, s // tm),
        in_specs=[pl.BlockSpec((1, tm, heads * LANES), row),
                  pl.BlockSpec((1, heads, tm, LANES), lambda i, m: (i, 0, m, 0)),
                  pl.BlockSpec((1, tm, dm - heads * LANES), row),
                  pl.BlockSpec((dm, d), fix),
                  pl.BlockSpec((1, tm, d), row),
                  pl.BlockSpec((1, 1, d), vec),
                  pl.BlockSpec((1, 1, d), vec),
                  pl.BlockSpec((1, 1, d), vec),
                  pl.BlockSpec((1, d), fix),
                  pl.BlockSpec((1, d), fix)],
        out_specs=[pl.BlockSpec((1, tm, d), row), pl.BlockSpec((1, tm, d), row)],
        out_shape=[jax.ShapeDtypeStruct((b, s, d), F32), jax.ShapeDtypeStruct((b, s, d), BF16)],
        compiler_params=_cparams(("parallel", "parallel")),
        name="out_proj",
    )(hs, plat, gdn, w_out, x, gm, scf, shf, nw1, nw2)


def _mlp_kernel(h_ref, w1_ref, w2_ref, x1_ref, gf_ref, nw_ref, o_ref):
    f = pl.program_id(2)

    @pl.when(f == 0)
    def _():
        o_ref[0] = jnp.zeros_like(o_ref[0])

    hid = jnp.maximum(_dot(h_ref[0], w1_ref[...]), 0.0)
    o_ref[0] = o_ref[0] + _dot((hid * hid).astype(BF16), w2_ref[...])

    @pl.when(f == pl.num_programs(2) - 1)
    def _():
        o_ref[0] = x1_ref[0] + gf_ref[0] * (_rms_rows(o_ref[0]) * nw_ref[...])


def _mlp(h2, w1, w2, x1, gf, nw3, tm, tf):
    b, s, d = x1.shape
    ff = w1.shape[1]
    row = lambda i, m, f: (i, m, 0)
    return pl.pallas_call(
        _mlp_kernel,
        grid=(b, s // tm, ff // tf),
        in_specs=[pl.BlockSpec((1, tm, d), row),
                  pl.BlockSpec((d, tf), lambda i, m, f: (0, f)),
                  pl.BlockSpec((tf, d), lambda i, m, f: (f, 0)),
                  pl.BlockSpec((1, tm, d), row),
                  pl.BlockSpec((1, 1, d), lambda i, m, f: (i, 0, 0)),
                  pl.BlockSpec((1, d), lambda i, m, f: (0, 0))],
        out_specs=pl.BlockSpec((1, tm, d), row),
        out_shape=jax.ShapeDtypeStruct((b, s, d), F32),
        compiler_params=_cparams(("parallel", "parallel", "arbitrary")),
        name="mlp",
    )(h2, w1, w2, x1, gf, nw3)


def kernel(x, c, ctx, c_ctx, w_mod, b_mod, norm_w, w_in, lru_conv_w, lru_conv_b, lru_gate_w, lru_gate_b,
           lru_lambda, gdn_conv_w, gdn_a_log, gdn_dt_bias, gdn_norm_w, w_out, w_ff1, w_ff2):
    b, s, d = x.shape
    lc = ctx.shape[1]
    l_all = lc + s
    rows = s // GRID_W
    d_lru = LRU_HEADS * LANES
    d_gdn = GDN_HEADS * LANES
    assert w_mod.shape[0] == 1 and b == SUBLANES and d == d_lru + d_gdn

    cc = jnp.concatenate([c, c_ctx[None], jnp.zeros((2 * SUBLANES - b - 1, d), F32)], axis=0)
    mod = _modulation(cc, w_mod[0], b_mod[0])
    sh_m, sc_m, g_m, sh_f, sc_f, g_f = [mod[:b, i * d:(i + 1) * d] for i in range(6)]
    csh_m = jnp.broadcast_to(mod[b:b + 1, 0:d], (b, d))
    csc_m = jnp.broadcast_to(mod[b:b + 1, d:2 * d], (b, d))
    nw = norm_w[0]
    nw0, nw1, nw2, nw3 = [nw[i:i + 1] for i in range(4)]
    v3 = lambda t: t.reshape(b, 1, d)

    w_in0 = w_in[0]
    y_end = 2 * d_lru
    qkv_end = y_end + 3 * d_gdn
    z_end = qkv_end + d_gdn
    w_bf = w_in0.astype(BF16)
    n_gb = w_in0.shape[1] - z_end
    w_gb = jnp.pad(w_in0[:, z_end:], ((0, 0), (0, LANES - n_gb))).astype(BF16)

    p_lat, gb_lat = _inproj_raster(x, nw0, v3(sc_m), v3(sh_m), w_bf, w_gb, col0=d_lru, n=z_end - d_lru,
                                   tm=1024, tn=512)
    p_ctx, gb_ctx = _inproj_raster(ctx, nw0, v3(csc_m), v3(csh_m), w_bf, w_gb, col0=y_end, n=qkv_end - y_end,
                                   tm=lc, tn=512)
    xl = _inproj_tb(x.reshape(b, rows, GRID_W, d), nw0, v3(sc_m), v3(sh_m), w_bf, n=d_lru, r=rows, nbh=b // 2)
    xl = xl.reshape(LRU_HEADS, GRID_W // SUBLANES, rows, SUBLANES * b, LANES)
    xc = _inproj_tb(ctx.reshape(b, lc // SUBLANES, SUBLANES, d), nw0, v3(csc_m), v3(csh_m), w_bf, n=d_lru,
                    r=2 * SUBLANES, nbh=b)

    gw = lru_gate_w[0]
    wg = (0.5 * jnp.transpose(gw, (2, 3, 0, 1, 4))).reshape(LRU_HEADS, LANES, 4 * LANES).astype(BF16)
    gbias = 0.5 * jnp.transpose(lru_gate_b[0].reshape(N_DIR, 2, LRU_HEADS, LANES), (2, 0, 1, 3))
    gbias = gbias.reshape(LRU_HEADS, 1, 4 * LANES)
    lam2 = jnp.transpose(lru_lambda[0].reshape(N_DIR, LRU_HEADS, LANES), (1, 0, 2)).reshape(1, 2 * d_lru)
    hs = _lru(xc, xl, lru_conv_w[0], lru_conv_b[0].reshape(1, d_lru), wg, gbias, lam2, b, lc, s)

    gb = jnp.concatenate([gb_ctx, gb_lat], axis=1)[:, :, :n_gb]
    gb = gb.reshape(b, l_all, 2, N_DIR, GDN_HEADS)
    grow = jnp.transpose(gb, (0, 4, 2, 3, 1)).reshape(b, GDN_HEADS, 2 * N_DIR, l_all)
    grow = jnp.pad(grow, ((0, 0), (0, 0), (0, SUBLANES - 2 * N_DIR), (0, 0)))
    def rows8(t):
        t = jnp.transpose(t, (1, 0))[:, :, None]
        t = jnp.pad(t, ((0, 0), (2, SUBLANES - 2 - N_DIR), (0, 0)))
        return jnp.broadcast_to(t, (GDN_HEADS, SUBLANES, LANES)).astype(F32)
    gdn = _gdn(p_ctx, p_lat, grow, rows8(gdn_a_log[0]), rows8(gdn_dt_bias[0]), gdn_conv_w[0],
               gdn_norm_w[0].reshape(1, LANES), lc, s)

    x1, h2 = _out_proj(hs, p_lat, gdn, w_out[0].astype(BF16), x, v3(g_m), v3(sc_f), v3(sh_f),
                       nw1, nw2, tm=512)
    return _mlp(h2, w_ff1[0].astype(BF16), w_ff2[0].astype(BF16), x1, v3(g_f), nw3, tm=512, tf=1024)
```

```python
import functools
import math

import jax
import jax.numpy as jnp
from jax import lax
from jax.experimental import pallas as pl
from jax.experimental.pallas import tpu as pltpu

F32 = jnp.float32
BF16 = jnp.bfloat16

LANES = 128
SUBLANES = 8
VMEM_LIMIT = 56 * 1024 * 1024

GRID_W = 64
NORM_EPS = 1e-6
LRU_C = 8.0
LRU_HEADS = 8
GDN_HEADS = 8
N_DIR = 2
CHUNK = 64
NEG_BIG = -1e30
OUT_SUB_ROWS = 256
LRU_SCAN_BLOCK = 8
GDN_HEADS_PER_STEP = 2
GDN_CHUNK_BATCH = 18


def _cparams(sem):
    return pltpu.CompilerParams(dimension_semantics=sem, vmem_limit_bytes=VMEM_LIMIT)


def _dot(a, b):
    return jnp.dot(a, b, preferred_element_type=F32)


def _dot_nt(a, b):
    return lax.dot_general(a, b, (((1,), (1,)), ((), ())), preferred_element_type=F32)


def _sigmoid(t):
    return 0.5 * jnp.tanh(0.5 * t) + 0.5


def _softplus(t):
    return jnp.maximum(t, 0.0) + jnp.log(1.0 + jnp.exp(-jnp.abs(t)))


def _silu(t):
    return t * _sigmoid(t)


def _gelu_tanh(t):
    return 0.5 * t * (1.0 + jnp.tanh(math.sqrt(2.0 / math.pi) * (t + 0.044715 * (t * t * t))))


def _rms_rows(t):
    return t * lax.rsqrt(jnp.mean(t * t, axis=-1, keepdims=True) + NORM_EPS)


def _mod_kernel(c_ref, w_ref, b_ref, o_ref):
    s = _silu(c_ref[...])
    o_ref[...] = _dot(s.astype(BF16), w_ref[...].astype(BF16)) + b_ref[...]


def _modulation(cc, w_mod, b_mod):
    rows, d = cc.shape
    n = w_mod.shape[1]
    tn = 1024
    return pl.pallas_call(
        _mod_kernel,
        grid=(n // tn,),
        in_specs=[pl.BlockSpec((rows, d), lambda j: (0, 0)),
                  pl.BlockSpec((d, tn), lambda j: (0, j)),
                  pl.BlockSpec((1, tn), lambda j: (0, j))],
        out_specs=pl.BlockSpec((rows, tn), lambda j: (0, j)),
        out_shape=jax.ShapeDtypeStruct((rows, n), F32),
        compiler_params=_cparams(("arbitrary",)),
        name="mod",
    )(cc, w_mod, b_mod.reshape(1, n))


def _inproj_raster_kernel(x_ref, nw_ref, sc_ref, sh_ref, w_ref, wgb_ref, p_ref, gb_ref, h_scr, *, tn):
    @pl.when(pl.program_id(2) == 0)
    def _():
        a = nw_ref[...] * (1.0 + sc_ref[0])
        h = (_rms_rows(x_ref[0]) * a + sh_ref[0]).astype(BF16)
        h_scr[...] = h
        gb_ref[0] = _dot(h, wgb_ref[...])

    acc = _dot(h_scr[...], w_ref[...])
    for j in range(tn // LANES):
        p_ref[0, j] = acc[:, j * LANES:(j + 1) * LANES].astype(p_ref.dtype)


def _inproj_raster(x, nw, sc, sh, w, wgb, col0, n, tm, tn):
    b, l, d = x.shape
    assert col0 % tn == 0 and n % tn == 0
    off = col0 // tn
    return pl.pallas_call(
        functools.partial(_inproj_raster_kernel, tn=tn),
        grid=(b, l // tm, n // tn),
        in_specs=[pl.BlockSpec((1, tm, d), lambda i, m, j: (i, m, 0)),
                  pl.BlockSpec((1, d), lambda i, m, j: (0, 0)),
                  pl.BlockSpec((1, 1, d), lambda i, m, j: (i, 0, 0)),
                  pl.BlockSpec((1, 1, d), lambda i, m, j: (i, 0, 0)),
                  pl.BlockSpec((d, tn), lambda i, m, j: (0, j + off)),
                  pl.BlockSpec((d, LANES), lambda i, m, j: (0, 0))],
        out_specs=[pl.BlockSpec((1, tn // LANES, tm, LANES), lambda i, m, j: (i, j, m, 0)),
                   pl.BlockSpec((1, tm, LANES), lambda i, m, j: (i, m, 0))],
        out_shape=[jax.ShapeDtypeStruct((b, n // LANES, l, LANES), BF16),
                   jax.ShapeDtypeStruct((b, l, LANES), F32)],
        scratch_shapes=[pltpu.VMEM((tm, d), BF16)],
        compiler_params=_cparams(("parallel", "parallel", "arbitrary")),
        name="inproj_raster",
    )(x, nw, sc, sh, w, wgb)


def _inproj_tb_kernel(x_ref, nw_ref, sc_ref, sh_ref, w_ref, o_ref, h_scr, *, nbh, nb):
    bh = pl.program_id(1)
    per_b = x_ref.shape[1] * x_ref.shape[2]
    d = x_ref.shape[3]
    for i in range(nbh):
        a = nw_ref[...] * (1.0 + sc_ref[i])
        h = _rms_rows(x_ref[i].reshape(per_b, d)) * a + sh_ref[i]
        h_scr[i * per_b:(i + 1) * per_b, :] = h.astype(BF16)
    acc = _dot(h_scr[...], w_ref[...])
    for j in range(w_ref.shape[1] // LANES):
        for i in range(nbh):
            o_ref[j, pl.ds(bh * nbh + i, per_b, stride=nb), :] = (
                acc[i * per_b:(i + 1) * per_b, j * LANES:(j + 1) * LANES])


def _inproj_tb(xv, nw, sc, sh, w, n, r, nbh):
    nb, r_total, c_total, d = xv.shape
    cw = SUBLANES
    assert nb % nbh == 0 and r_total % r == 0 and c_total % cw == 0 and (c_total == cw or r_total == r)
    per_b = r * cw
    n_m = (r_total // r) * (c_total // cw)
    x_map = (lambda m, bh: (bh, m, 0, 0)) if c_total == cw else (lambda m, bh: (bh, 0, m, 0))
    return pl.pallas_call(
        functools.partial(_inproj_tb_kernel, nbh=nbh, nb=nb),
        grid=(n_m, nb // nbh),
        in_specs=[pl.BlockSpec((nbh, r, cw, d), x_map),
                  pl.BlockSpec((1, d), lambda m, bh: (0, 0)),
                  pl.BlockSpec((nbh, 1, d), lambda m, bh: (bh, 0, 0)),
                  pl.BlockSpec((nbh, 1, d), lambda m, bh: (bh, 0, 0)),
                  pl.BlockSpec((d, n), lambda m, bh: (0, 0))],
        out_specs=pl.BlockSpec((n // LANES, per_b * nb, LANES), lambda m, bh: (0, m, 0)),
        out_shape=jax.ShapeDtypeStruct((n // LANES, n_m * per_b * nb, LANES), F32),
        scratch_shapes=[pltpu.VMEM((nbh * per_b, d), BF16)],
        compiler_params=_cparams(("parallel", "arbitrary")),
        name="inproj_tb",
    )(xv, nw, sc, sh, w)


def _lru_kernel(xc_ref, xl_ref, cw_ref, cb_ref, wg_ref, gbias_ref, lam_ref, o_ref,
                hsum, af, uf, ab, ub, *, lc, s, rows, nb):
    hd = LANES
    cpb = SUBLANES
    tseg = cpb * rows
    nblk = GRID_W // cpb
    cw = cw_ref[...]
    cb = cb_ref[...]
    half_nsp = (-0.5 * LRU_C) * _softplus(-lam_ref[...])

    def conv(prev, main, nxt):
        xm2 = jnp.concatenate([prev, main[:-2 * nb]], axis=0)
        xm1 = jnp.concatenate([prev[nb:], main[:-nb]], axis=0)
        xp1 = jnp.concatenate([main[nb:], nxt], axis=0)
        return cw[0:1] * xm2 + cw[1:2] * xm1 + cw[2:3] * main + cw[3:4] * xp1 + cb

    def ctx_xr(t0):
        ref = xc_ref.at[0]
        main = ref[pl.ds(pl.multiple_of(t0 * nb, SUBLANES), tseg * nb), :]
        p0 = jnp.maximum(t0 * nb - 2 * nb, 0)
        prev = ref[pl.ds(pl.multiple_of(p0, SUBLANES), 2 * nb), :]
        n0 = jnp.minimum((t0 + tseg) * nb, lc * nb - nb)
        nxt = ref[pl.ds(pl.multiple_of(n0, SUBLANES), nb), :]
        return conv(jnp.where(t0 > 0, prev, 0.0), main, jnp.where(t0 + tseg < lc, nxt, 0.0))

    def lat_xr(sg):
        main = jnp.concatenate([xl_ref[0, sg, :, w * nb:(w + 1) * nb, :].reshape(rows * nb, hd)
                                for w in range(cpb)], axis=0)
        prev = xl_ref[0, jnp.maximum(sg - 1, 0), rows - 2:rows, (cpb - 1) * nb:cpb * nb, :].reshape(2 * nb, hd)
        nxt = xl_ref[0, jnp.minimum(sg + 1, nblk - 1), 0:1, 0:nb, :].reshape(nb, hd)
        return conv(jnp.where(sg > 0, prev, 0.0), main, jnp.where(sg < nblk - 1, nxt, 0.0))

    def gates(xr, direction, a_ref, u_ref):
        lo = direction * 2 * hd
        th_g = jnp.tanh(_dot(xr.astype(BF16), wg_ref[0, :, lo:lo + 2 * hd]) + gbias_ref[0, :, lo:lo + 2 * hd])
        hnsp = half_nsp[:, direction * hd:(direction + 1) * hd]
        log_a = th_g[:, :hd] * hnsp + hnsp
        ig = 0.5 * th_g[:, hd:] + 0.5
        a = jnp.exp(log_a)
        a_ref[...] = a
        sq = -jnp.tanh(log_a) * (1.0 + a * a)
        root = jnp.where(sq > 0.0, sq * lax.rsqrt(sq), 0.0)
        u_ref[...] = root * (ig * xr)

    def scan_segment(hf, hb, store, fseg, bseg):
        for blk in range(tseg // LRU_SCAN_BLOCK):
            for direction in range(N_DIR):
                a_ref, u_ref = (af, uf) if direction == 0 else (ab, ub)
                h0 = hf if direction == 0 else hb
                sg = fseg if direction == 0 else bseg
                pa = pu = None
                for kk in range(LRU_SCAN_BLOCK):
                    step = blk * LRU_SCAN_BLOCK + kk
                    t = step if direction == 0 else tseg - 1 - step
                    a = a_ref[t * nb:(t + 1) * nb, :]
                    u = u_ref[t * nb:(t + 1) * nb, :]
                    pa, pu = (a, u) if kk == 0 else (a * pa, a * pu + u)
                    h = pa * h0 + pu
                    w, r = t // rows, t % rows
                    if store == "set":
                        hsum[sg, r, w * nb:(w + 1) * nb, :] = h
                    elif store == "add":
                        hsum[sg, r, w * nb:(w + 1) * nb, :] = hsum[sg, r, w * nb:(w + 1) * nb, :] + h
                if direction == 0:
                    hf = h
                else:
                    hb = h
        return hf, hb

    h0 = jnp.zeros((nb, hd), F32)

    nseg_c = lc // tseg
    def ctx_body(i, carry):
        gates(ctx_xr(i * tseg), 0, af, uf)
        gates(ctx_xr((nseg_c - 1 - i) * tseg), 1, ab, ub)
        return scan_segment(carry[0], carry[1], None, 0, 0)
    hf, hb = lax.fori_loop(0, nseg_c, ctx_body, (h0, h0))

    def lat_body(store):
        def body(i, carry):
            bseg = nblk - 1 - i
            gates(lat_xr(i), 0, af, uf)
            gates(lat_xr(bseg), 1, ab, ub)
            return scan_segment(carry[0], carry[1], store, i, bseg)
        return body
    hf, hb = lax.fori_loop(0, nblk // 2, lat_body("set"), (hf, hb))
    lax.fori_loop(nblk // 2, nblk, lat_body("add"), (hf, hb))

    def out_body(rr, carry):
        for i in range(nb):
            for wg in range(nblk):
                o_ref[i, pl.ds(pl.multiple_of(rr * GRID_W + wg * cpb, SUBLANES), cpb), :] = (
                    hsum[wg, rr, pl.ds(i, cpb, stride=nb), :])
        return carry
    lax.fori_loop(0, rows, out_body, 0)


def _lru(xc, xl, conv_w, conv_b, wg, gbias, lam2, nb, lc, s):
    heads = xl.shape[0]
    rows = s // GRID_W
    tseg = SUBLANES * rows
    nblk = GRID_W // SUBLANES
    assert lc % tseg == 0 and nblk % 2 == 0 and tseg % LRU_SCAN_BLOCK == 0 and nb == SUBLANES
    return pl.pallas_call(
        functools.partial(_lru_kernel, lc=lc, s=s, rows=rows, nb=nb),
        grid=(heads,),
        in_specs=[pl.BlockSpec((1, lc * nb, LANES), lambda h: (h, 0, 0)),
                  pl.BlockSpec((1, nblk, rows, SUBLANES * nb, LANES), lambda h: (h, 0, 0, 0, 0)),
                  pl.BlockSpec((4, LANES), lambda h: (0, h)),
                  pl.BlockSpec((1, LANES), lambda h: (0, h)),
                  pl.BlockSpec((1, LANES, 4 * LANES), lambda h: (h, 0, 0)),
                  pl.BlockSpec((1, 1, 4 * LANES), lambda h: (h, 0, 0)),
                  pl.BlockSpec((1, 2 * LANES), lambda h: (0, h))],
        out_specs=pl.BlockSpec((nb, s, LANES), lambda h: (0, 0, h)),
        out_shape=jax.ShapeDtypeStruct((nb, s, heads * LANES), F32),
        scratch_shapes=[pltpu.VMEM((nblk, rows, SUBLANES * nb, LANES), F32)]
                       + [pltpu.VMEM((tseg * nb, LANES), F32)] * 4,
        compiler_params=_cparams(("parallel",)),
        name="lru",
    )(xc, xl, conv_w, conv_b, wg, gbias, lam2)


def _bmm(a, b):
    return lax.dot_general(a, b, (((2,), (1,)), ((0,), (0,))), preferred_element_type=F32)


def _bmm_nt(a, b):
    return lax.dot_general(a, b, (((2,), (2,)), ((0,), (0,))), preferred_element_type=F32)


def _inv_unit_triangular_x4(lm):
    n, c, w = lm.shape
    nblk = w // c
    ri = lax.broadcasted_iota(jnp.int32, (w, w), 0)
    ci = lax.broadcasted_iota(jnp.int32, (w, w), 1)
    on_diag_block = (ri // c) == (ci // c)
    eye = (lax.broadcasted_iota(jnp.int32, (c, w), 0)
           == lax.broadcasted_iota(jnp.int32, (c, w), 1) % c).astype(F32)

    def block_diag(x):
        return jnp.where(on_diag_block, jnp.concatenate([x] * nblk, axis=1), 0.0).astype(BF16)

    p = eye - lm
    lk = _bmm(lm.astype(BF16), block_diag(lm))
    n_sq = int(math.log2(c)) - 2
    for _ in range(n_sq):
        x = _bmm(jnp.concatenate([p, lk], axis=1).astype(BF16), block_diag(lk))
        p = p + x[:, :c]
        lk = x[:, c:]
    return p + _bmm(p.astype(BF16), block_diag(lk))


def _gdn_kernel(qc_ref, kc_ref, vc_ref, ql_ref, kl_ref, vl_ref, z_ref, g_ref, alog_ref, dtb_ref,
                cwq_ref, cwk_ref, cwv_ref, nw_ref, o_ref,
                qs, ks, vs, qcol, grow, grow2, xpad, aq_s, b_s, op_s, st_s, o_s, *, lc, s, hg, nbc):
    dk = LANES
    c = CHUNK
    l_all = lc + s
    ncc = lc // c
    ncl = s // c
    nct = ncc + ncl
    npb = nbc // 2

    ii = lax.broadcasted_iota(jnp.int32, (c, 2 * c), 0)
    jl = lax.broadcasted_iota(jnp.int32, (c, 2 * c), 1)
    fwd_half = jl < c
    jj = jnp.where(fwd_half, jl, jl - c)
    keep2 = (fwd_half & (ii >= jj)) | ((jl >= c) & (ii <= jj))
    strict2 = (fwd_half & (ii > jj)) | ((jl >= c) & (ii < jj))
    lane1 = lax.broadcasted_iota(jnp.int32, (1, LANES), 1)
    eye_k = lax.broadcasted_iota(jnp.int32, (dk, dk), 0) == lax.broadcasted_iota(jnp.int32, (dk, dk), 1)
    blk = 2 * LANES
    si = lax.broadcasted_iota(jnp.int32, (blk, blk), 0)
    ji = lax.broadcasted_iota(jnp.int32, (blk, blk), 1)
    same = (si // c) == (ji // c)
    t_pre = jnp.where(same & (si <= ji), 1.0, 0.0).astype(F32)
    t_suf = jnp.where(same & (si >= ji), 1.0, 0.0).astype(F32)

    def conv_silu(ref, hd, cw, n):
        p = SUBLANES
        zero = jnp.zeros((p, dk), F32)
        xpad[0:p] = zero
        xpad[p + n:2 * p + n] = zero
        xpad[p:p + n] = ref[0, hd].astype(F32)
        return _silu(cw[0:1] * xpad[p - 2:p - 2 + n] + cw[1:2] * xpad[p - 1:p - 1 + n]
                     + cw[2:3] * xpad[p:p + n] + cw[3:4] * xpad[p + 1:p + 1 + n])

    def l2n(t):
        return t * lax.rsqrt(jnp.sum(t * t, axis=-1, keepdims=True) + NORM_EPS)

    for hd in range(hg):
        cwq = cwq_ref[:, hd * dk:(hd + 1) * dk]
        cwk = cwk_ref[:, hd * dk:(hd + 1) * dk]
        cwv = cwv_ref[:, hd * dk:(hd + 1) * dk]
        qs[0:lc] = l2n(conv_silu(qc_ref, hd, cwq, lc)) * (dk ** -0.5)
        qs[lc:l_all] = l2n(conv_silu(ql_ref, hd, cwq, s)) * (dk ** -0.5)
        ks[0:lc] = l2n(conv_silu(kc_ref, hd, cwk, lc))
        ks[lc:l_all] = l2n(conv_silu(kl_ref, hd, cwk, s))
        vs[0:lc] = conv_silu(vc_ref, hd, cwv, lc)
        vs[lc:l_all] = conv_silu(vl_ref, hd, cwv, s)

        r = g_ref[0, hd]
        rid8 = lax.broadcasted_iota(jnp.int32, (SUBLANES, l_all), 0)
        aneg = -jnp.exp(alog_ref[hd][:, 0:1])
        dtb = dtb_ref[hd][:, 0:1]
        val = jnp.where(rid8 < 2, _sigmoid(r), aneg * _softplus(r + dtb))
        for i in range(l_all // blk):
            vb = val[:, i * blk:(i + 1) * blk]
            pre = jnp.dot(vb, t_pre, preferred_element_type=F32, precision=lax.Precision.HIGHEST)
            suf = jnp.dot(vb, t_suf, preferred_element_type=F32, precision=lax.Precision.HIGHEST)
            rb = lax.broadcasted_iota(jnp.int32, (SUBLANES, blk), 0)
            gblk = jnp.where(rb == 2, pre, jnp.where(rb == 3, suf, vb))
            grow[:, i * blk:(i + 1) * blk] = gblk
            for p2 in range(blk // LANES):
                g128 = gblk[:, p2 * LANES:(p2 + 1) * LANES]
                rolled = pltpu.roll(g128, c, 1)
                ch = (i * (blk // LANES) + p2) * 2
                grow2[0:1, ch * LANES:(ch + 1) * LANES] = jnp.where(lane1 < c, g128[2:3], rolled[3:4])
                grow2[0:1, (ch + 1) * LANES:(ch + 2) * LANES] = jnp.where(lane1 < c, rolled[2:3], g128[3:4])
        gfull = jnp.concatenate([grow[...], jnp.zeros((LANES - SUBLANES, l_all), F32)], axis=0)
        qcol[...] = gfull.T

        def intra(it, carry, hd=hd):
            r0 = pl.multiple_of(it * (nbc * c), 2 * c)
            q = qs[pl.ds(r0, nbc * c), :].reshape(nbc, c, dk)
            k = ks[pl.ds(r0, nbc * c), :].reshape(nbc, c, dk)
            v = vs[pl.ds(r0, nbc * c), :].reshape(nbc, c, dk)
            col = qcol[pl.ds(r0, nbc * c), :].reshape(nbc, c, LANES)
            g2 = grow2[0:1, pl.ds(pl.multiple_of(it * (nbc * LANES), LANES), nbc * LANES)]
            grw2 = jnp.stack([g2[:, j * LANES:(j + 1) * LANES] for j in range(nbc)], axis=0)
            beta = [col[:, :, d:d + 1] for d in range(N_DIR)]
            gcol = [col[:, :, 2 + d:3 + d] for d in range(N_DIR)]
            beta2 = jnp.where(fwd_half, beta[0], beta[1])
            gcol2 = jnp.where(fwd_half, gcol[0], gcol[1])
            kb16 = k.astype(BF16)
            both = _bmm_nt(jnp.concatenate([q.astype(BF16), kb16], axis=1),
                           jnp.concatenate([kb16, kb16], axis=1))
            decay2 = jnp.exp(jnp.where(keep2, gcol2 - grw2, NEG_BIG))
            lm2 = jnp.where(strict2, beta2 * both[:, c:] * decay2, 0.0)
            qkd2 = (both[:, :c] * decay2).astype(BF16)
            half = nbc // 2
            tinv4 = _inv_unit_triangular_x4(jnp.concatenate([lm2[:half], lm2[half:]], axis=2))
            tinv2 = jnp.concatenate([tinv4[:, :, :2 * c], tinv4[:, :, 2 * c:]], axis=0)
            eg = [jnp.exp(g) for g in gcol]
            glast = [gcol[0][:, c - 1:c, :], gcol[1][:, 0:1, :]]
            rhs = [jnp.concatenate([v * beta[d], (k * beta[d]) * eg[d]], axis=2).astype(BF16) for d in range(N_DIR)]
            zero = jnp.zeros((nbc, c, 2 * dk), BF16)
            rhs_bd = jnp.concatenate([jnp.concatenate([rhs[0], zero], axis=2),
                                      jnp.concatenate([zero, rhs[1]], axis=2)], axis=1)
            uw2 = _bmm(tinv2.astype(BF16), rhs_bd).astype(BF16)
            uw_bd = jnp.concatenate([jnp.concatenate([uw2[:, :, :2 * dk], zero], axis=2),
                                     jnp.concatenate([zero, uw2[:, :, 2 * dk:]], axis=2)], axis=1)
            kd = jnp.concatenate([k * jnp.exp(glast[d] - gcol[d]) for d in range(N_DIR)], axis=1)
            kdt2 = jnp.swapaxes(kd, 1, 2).astype(BF16)
            m2 = _bmm(jnp.concatenate([kdt2, qkd2], axis=1), uw_bd)
            def split_pairs(t):
                t = t.reshape((npb, 2) + t.shape[1:])
                return t[:, 0], t[:, 1]
            for d in range(N_DIR):
                m = m2[:, :, d * 2 * dk:(d + 1) * 2 * dk]
                a_mat = jnp.where(eye_k, jnp.exp(glast[d]), 0.0) - m[:, :dk, dk:]
                q_mat = q * eg[d] - m[:, dk:, dk:]
                parts = [split_pairs(t) for t in (a_mat, m[:, :dk, :dk], q_mat, m[:, dk:, :dk])]
                (a1, b1, q1, o1), (a2, b2, q2, o2) = [[p[first ^ d] for p in parts] for first in (0, 1)]
                mm = _bmm(jnp.concatenate([a2, q2], axis=1).astype(BF16),
                          jnp.concatenate([a1, b1], axis=2).astype(BF16))
                q21 = mm[:, dk:, :dk]
                o21 = mm[:, dk:, dk:] + o2
                q_tok = (q1, q21) if d == 0 else (q21, q1)
                o_tok = (o1, o21) if d == 0 else (o21, o1)
                aq_s[hd, d, pl.ds(it * npb, npb)] = jnp.concatenate((mm[:, :dk, :dk],) + q_tok, axis=1).astype(BF16)
                b_s[hd, d, pl.ds(it * npb, npb)] = (mm[:, :dk, dk:] + b2).astype(BF16)
                op_s[hd, d, pl.ds(it * npb, npb)] = jnp.concatenate(o_tok, axis=1).astype(BF16)
            return carry
        for it in range(nct // nbc):
            intra(it, 0)

    st_s[...] = jnp.zeros_like(st_s)
    o_s[...] = jnp.zeros_like(o_s)
    ncp = ncc // 2
    nlp = ncl // 2

    def recur(t, carry):
        pf = t
        pb = jnp.where(t < ncp, ncp - 1 - t, 2 * ncp + nlp - 1 - t)
        chains = [(hd, d, pi) for hd in range(hg) for d, pi in ((0, pf), (1, pb))]
        loaded = []
        for hd, d, pi in chains:
            ro = pl.multiple_of(jnp.where(t >= ncp, pi * (2 * c) - lc, s), 2 * c)
            loaded.append((aq_s[hd, d, pi], st_s[hd, d], b_s[hd, d, pi], op_s[hd, d, pi],
                           o_s[hd, pl.ds(ro, 2 * c), :], ro))
        results = []
        for aq, st, bm, om, o_old, ro in loaded:
            x = _dot(aq, st.astype(BF16))
            results.append((x[:dk] + bm.astype(F32), o_old + x[dk:] + om.astype(F32), ro))
        for (hd, d, pi), (st_new, o_new, ro) in zip(chains, results):
            st_s[hd, d] = st_new
            o_s[hd, pl.ds(ro, 2 * c), :] = o_new
        return carry
    lax.fori_loop(0, ncp + nlp, recur, 0)

    for hd in range(hg):
        z = z_ref[0, hd].astype(F32)
        o_ref[0, :, hd * dk:(hd + 1) * dk] = (
            (_rms_rows(o_s[hd, 0:s]) * nw_ref[...]).astype(F32) * _silu(z)).astype(o_ref.dtype)


def _gdn(pc, plat, grow, alog8, dtb8, conv_w, norm_w, lc, s):
    b = plat.shape[0]
    heads = GDN_HEADS
    l_all = lc + s
    nct = l_all // CHUNK
    assert lc % CHUNK == 0 and s % CHUNK == 0 and l_all % (2 * LANES) == 0
    assert GDN_CHUNK_BATCH % 2 == 0 and nct % GDN_CHUNK_BATCH == 0
    assert lc % (2 * CHUNK) == 0 and s % (4 * CHUNK) == 0
    hg = GDN_HEADS_PER_STEP
    assert heads % hg == 0
    def slot(base):
        return lambda i, h: (i, base // hg + h, 0, 0)
    def cw(base):
        return lambda i, h: (0, base // hg + h)
    return pl.pallas_call(
        functools.partial(_gdn_kernel, lc=lc, s=s, hg=hg, nbc=GDN_CHUNK_BATCH),
        grid=(b, heads // hg),
        in_specs=[pl.BlockSpec((1, hg, lc, LANES), slot(0)),
                  pl.BlockSpec((1, hg, lc, LANES), slot(heads)),
                  pl.BlockSpec((1, hg, lc, LANES), slot(2 * heads)),
                  pl.BlockSpec((1, hg, s, LANES), slot(heads)),
                  pl.BlockSpec((1, hg, s, LANES), slot(2 * heads)),
                  pl.BlockSpec((1, hg, s, LANES), slot(3 * heads)),
                  pl.BlockSpec((1, hg, s, LANES), slot(4 * heads)),
                  pl.BlockSpec((1, hg, SUBLANES, l_all), lambda i, h: (i, h, 0, 0)),
                  pl.BlockSpec((hg, SUBLANES, LANES), lambda i, h: (h, 0, 0)),
                  pl.BlockSpec((hg, SUBLANES, LANES), lambda i, h: (h, 0, 0)),
                  pl.BlockSpec((4, hg * LANES), cw(0)),
                  pl.BlockSpec((4, hg * LANES), cw(heads)),
                  pl.BlockSpec((4, hg * LANES), cw(2 * heads)),
                  pl.BlockSpec((1, LANES), lambda i, h: (0, 0))],
        out_specs=pl.BlockSpec((1, s, hg * LANES), lambda i, h: (i, 0, h)),
        out_shape=jax.ShapeDtypeStruct((b, s, heads * LANES), BF16),
        scratch_shapes=[pltpu.VMEM((l_all, LANES), F32),
                        pltpu.VMEM((l_all, LANES), F32),
                        pltpu.VMEM((l_all, LANES), F32),
                        pltpu.VMEM((l_all, LANES), F32),
                        pltpu.VMEM((SUBLANES, l_all), F32),
                        pltpu.VMEM((SUBLANES, nct * LANES), F32),
                        pltpu.VMEM((s + 2 * SUBLANES, LANES), F32),
                        pltpu.VMEM((hg, N_DIR, nct // 2, LANES + 2 * CHUNK, LANES), BF16),
                        pltpu.VMEM((hg, N_DIR, nct // 2, LANES, LANES), BF16),
                        pltpu.VMEM((hg, N_DIR, nct // 2, 2 * CHUNK, LANES), BF16),
                        pltpu.VMEM((hg, N_DIR, LANES, LANES), F32),
                        pltpu.VMEM((hg, s + 2 * CHUNK, LANES), F32)],
        compiler_params=_cparams(("parallel", "arbitrary")),
        name="gdn",
    )(pc, pc, pc, plat, plat, plat, plat, grow, alog8, dtb8, conv_w, conv_w, conv_w, norm_w)


def _out_kernel(hs_ref, y_ref, gdn_ref, w_ref, x_ref, gm_ref, scf_ref, shf_ref, nw1_ref, nw2_ref,
                x1_ref, h2_ref, *, heads):
    half = heads * LANES
    a2 = nw2_ref[...] * (1.0 + scf_ref[0])
    tm = x_ref.shape[1]
    for r0 in range(0, tm, OUT_SUB_ROWS):
        rs = slice(r0, r0 + OUT_SUB_ROWS)
        y = jnp.concatenate([y_ref[0, j, rs, :] for j in range(heads)], axis=-1).astype(F32)
        lru = (hs_ref[0, rs, :] * _gelu_tanh(y)).astype(BF16)
        m = _dot(lru, w_ref[0:half, :]) + _dot(gdn_ref[0, rs, :], w_ref[half:, :])
        x1 = x_ref[0, rs, :] + gm_ref[0] * (_rms_rows(m) * nw1_ref[...])
        x1_ref[0, rs, :] = x1
        h2_ref[0, rs, :] = (_rms_rows(x1) * a2 + shf_ref[0]).astype(BF16)


def _out_proj(hs, plat, gdn, w_out, x, gm, scf, shf, nw1, nw2, tm):
    b, s, d = x.shape
    heads = LRU_HEADS
    dm = w_out.shape[0]
    row = lambda i, m: (i, m, 0)
    vec = lambda i, m: (i, 0, 0)
    fix = lambda i, m: (0, 0)
    return pl.pallas_call(
        functools.partial(_out_kernel, heads=heads),
        grid=(b, s // tm),
        in_specs=[pl.BlockSpec((1, tm, heads * LANES), row),
                  pl.BlockSpec((1, heads, tm, LANES), lambda i, m: (i, 0, m, 0)),
                  pl.BlockSpec((1, tm, dm - heads * LANES), row),
                  pl.BlockSpec((dm, d), fix),
                  pl.BlockSpec((1, tm, d), row),
                  pl.BlockSpec((1, 1, d), vec),
                  pl.BlockSpec((1, 1, d), vec),
                  pl.BlockSpec((1, 1, d), vec),
                  pl.BlockSpec((1, d), fix),
                  pl.BlockSpec((1, d), fix)],
        out_specs=[pl.BlockSpec((1, tm, d), row), pl.BlockSpec((1, tm, d), row)],
        out_shape=[jax.ShapeDtypeStruct((b, s, d), F32), jax.ShapeDtypeStruct((b, s, d), BF16)],
        compiler_params=_cparams(("parallel", "parallel")),
        name="out_proj",
    )(hs, plat, gdn, w_out, x, gm, scf, shf, nw1, nw2)


def _mlp_kernel(h_ref, w1_ref, w2_ref, x1_ref, gf_ref, nw_ref, o_ref):
    f = pl.program_id(2)

    @pl.when(f == 0)
    def _():
        o_ref[0] = jnp.zeros_like(o_ref[0])

    hid = jnp.maximum(_dot(h_ref[0], w1_ref[...]), 0.0)
    o_ref[0] = o_ref[0] + _dot((hid * hid).astype(BF16), w2_ref[...])

    @pl.when(f == pl.num_programs(2) - 1)
    def _():
        o_ref[0] = x1_ref[0] + gf_ref[0] * (_rms_rows(o_ref[0]) * nw_ref[...])


def _mlp(h2, w1, w2, x1, gf, nw3, tm, tf):
    b, s, d = x1.shape
    ff = w1.shape[1]
    row = lambda i, m, f: (i, m, 0)
    return pl.pallas_call(
        _mlp_kernel,
        grid=(b, s // tm, ff // tf),
        in_specs=[pl.BlockSpec((1, tm, d), row),
                  pl.BlockSpec((d, tf), lambda i, m, f: (0, f)),
                  pl.BlockSpec((tf, d), lambda i, m, f: (f, 0)),
                  pl.BlockSpec((1, tm, d), row),
                  pl.BlockSpec((1, 1, d), lambda i, m, f: (i, 0, 0)),
                  pl.BlockSpec((1, d), lambda i, m, f: (0, 0))],
        out_specs=pl.BlockSpec((1, tm, d), row),
        out_shape=jax.ShapeDtypeStruct((b, s, d), F32),
        compiler_params=_cparams(("parallel", "parallel", "arbitrary")),
        name="mlp",
    )(h2, w1, w2, x1, gf, nw3)


def kernel(x, c, ctx, c_ctx, w_mod, b_mod, norm_w, w_in, lru_conv_w, lru_conv_b, lru_gate_w, lru_gate_b,
           lru_lambda, gdn_conv_w, gdn_a_log, gdn_dt_bias, gdn_norm_w, w_out, w_ff1, w_ff2):
    b, s, d = x.shape
    lc = ctx.shape[1]
    l_all = lc + s
    rows = s // GRID_W
    d_lru = LRU_HEADS * LANES
    d_gdn = GDN_HEADS * LANES
    assert w_mod.shape[0] == 1 and b == SUBLANES and d == d_lru + d_gdn

    cc = jnp.concatenate([c, c_ctx[None], jnp.zeros((2 * SUBLANES - b - 1, d), F32)], axis=0)
    mod = _modulation(cc, w_mod[0], b_mod[0])
    sh_m, sc_m, g_m, sh_f, sc_f, g_f = [mod[:b, i * d:(i + 1) * d] for i in range(6)]
    csh_m = jnp.broadcast_to(mod[b:b + 1, 0:d], (b, d))
    csc_m = jnp.broadcast_to(mod[b:b + 1, d:2 * d], (b, d))
    nw = norm_w[0]
    nw0, nw1, nw2, nw3 = [nw[i:i + 1] for i in range(4)]
    v3 = lambda t: t.reshape(b, 1, d)

    w_in0 = w_in[0]
    y_end = 2 * d_lru
    qkv_end = y_end + 3 * d_gdn
    z_end = qkv_end + d_gdn
    w_bf = w_in0.astype(BF16)
    n_gb = w_in0.shape[1] - z_end
    w_gb = jnp.pad(w_in0[:, z_end:], ((0, 0), (0, LANES - n_gb))).astype(BF16)

    p_lat, gb_lat = _inproj_raster(x, nw0, v3(sc_m), v3(sh_m), w_bf, w_gb, col0=d_lru, n=z_end - d_lru,
                                   tm=1024, tn=1024)
    p_ctx, gb_ctx = _inproj_raster(ctx, nw0, v3(csc_m), v3(csh_m), w_bf, w_gb, col0=y_end, n=qkv_end - y_end,
                                   tm=lc, tn=1024)
    xl = _inproj_tb(x.reshape(b, rows, GRID_W, d), nw0, v3(sc_m), v3(sh_m), w_bf, n=d_lru, r=rows, nbh=b // 2)
    xl = xl.reshape(LRU_HEADS, GRID_W // SUBLANES, rows, SUBLANES * b, LANES)
    xc = _inproj_tb(ctx.reshape(b, lc // SUBLANES, SUBLANES, d), nw0, v3(csc_m), v3(csh_m), w_bf, n=d_lru,
                    r=2 * SUBLANES, nbh=b)

    gw = lru_gate_w[0]
    wg = (0.5 * jnp.transpose(gw, (2, 3, 0, 1, 4))).reshape(LRU_HEADS, LANES, 4 * LANES).astype(BF16)
    gbias = 0.5 * jnp.transpose(lru_gate_b[0].reshape(N_DIR, 2, LRU_HEADS, LANES), (2, 0, 1, 3))
    gbias = gbias.reshape(LRU_HEADS, 1, 4 * LANES)
    lam2 = jnp.transpose(lru_lambda[0].reshape(N_DIR, LRU_HEADS, LANES), (1, 0, 2)).reshape(1, 2 * d_lru)
    hs = _lru(xc, xl, lru_conv_w[0], lru_conv_b[0].reshape(1, d_lru), wg, gbias, lam2, b, lc, s)

    gb = jnp.concatenate([gb_ctx, gb_lat], axis=1)[:, :, :n_gb]
    gb = gb.reshape(b, l_all, 2, N_DIR, GDN_HEADS)
    grow = jnp.transpose(gb, (0, 4, 2, 3, 1)).reshape(b, GDN_HEADS, 2 * N_DIR, l_all)
    grow = jnp.pad(grow, ((0, 0), (0, 0), (0, SUBLANES - 2 * N_DIR), (0, 0)))
    def rows8(t):
        t = jnp.transpose(t, (1, 0))[:, :, None]
        t = jnp.pad(t, ((0, 0), (2, SUBLANES - 2 - N_DIR), (0, 0)))
        return jnp.broadcast_to(t, (GDN_HEADS, SUBLANES, LANES)).astype(F32)
    gdn = _gdn(p_ctx, p_lat, grow, rows8(gdn_a_log[0]), rows8(gdn_dt_bias[0]), gdn_conv_w[0],
               gdn_norm_w[0].reshape(1, LANES), lc, s)

    x1, h2 = _out_proj(hs, p_lat, gdn, w_out[0].astype(BF16), x, v3(g_m), v3(sc_f), v3(sh_f),
                       nw1, nw2, tm=512)
    return _mlp(h2, w_ff1[0].astype(BF16), w_ff2[0].astype(BF16), x1, v3(g_f), nw3, tm=512, tf=1024)
```

```python
import functools
import math

import jax
import jax.numpy as jnp
from jax import lax
from jax.experimental import pallas as pl
from jax.experimental.pallas import tpu as pltpu

F32 = jnp.float32
BF16 = jnp.bfloat16

LANES = 128
SUBLANES = 8
VMEM_LIMIT = 56 * 1024 * 1024
VMEM_LIMIT_MLP = 62 * 1024 * 1024

GRID_W = 64
NORM_EPS = 1e-6
LRU_C = 8.0
LRU_HEADS = 8
GDN_HEADS = 8
N_DIR = 2
CHUNK = 64
NEG_BIG = -1e30
OUT_SUB_ROWS = 256
LRU_SCAN_BLOCK = 8
GDN_HEADS_PER_STEP = 2
GDN_CHUNK_BATCH = 18


def _cparams(sem, vmem_limit=VMEM_LIMIT):
    return pltpu.CompilerParams(dimension_semantics=sem, vmem_limit_bytes=vmem_limit)


def _dot(a, b):
    return jnp.dot(a, b, preferred_element_type=F32)


def _dot_nt(a, b):
    return lax.dot_general(a, b, (((1,), (1,)), ((), ())), preferred_element_type=F32)


def _sigmoid(t):
    return 0.5 * jnp.tanh(0.5 * t) + 0.5


def _softplus(t):
    return jnp.maximum(t, 0.0) + jnp.log(1.0 + jnp.exp(-jnp.abs(t)))


def _silu(t):
    return t * _sigmoid(t)


def _gelu_tanh(t):
    return 0.5 * t * (1.0 + jnp.tanh(math.sqrt(2.0 / math.pi) * (t + 0.044715 * (t * t * t))))


def _rms_rows(t):
    return t * lax.rsqrt(jnp.mean(t * t, axis=-1, keepdims=True) + NORM_EPS)


def _mod_kernel(c_ref, w_ref, b_ref, o_ref):
    s = _silu(c_ref[...])
    o_ref[...] = _dot(s.astype(BF16), w_ref[...].astype(BF16)) + b_ref[...]


def _modulation(cc, w_mod, b_mod):
    rows, d = cc.shape
    n = w_mod.shape[1]
    tn = 1024
    return pl.pallas_call(
        _mod_kernel,
        grid=(n // tn,),
        in_specs=[pl.BlockSpec((rows, d), lambda j: (0, 0)),
                  pl.BlockSpec((d, tn), lambda j: (0, j)),
                  pl.BlockSpec((1, tn), lambda j: (0, j))],
        out_specs=pl.BlockSpec((rows, tn), lambda j: (0, j)),
        out_shape=jax.ShapeDtypeStruct((rows, n), F32),
        compiler_params=_cparams(("arbitrary",)),
        name="mod",
    )(cc, w_mod, b_mod.reshape(1, n))


def _inproj_raster_kernel(x_ref, nw_ref, sc_ref, sh_ref, w_ref, wgb_ref, p_ref, gb_ref, h_scr, *, tn):
    @pl.when(pl.program_id(2) == 0)
    def _():
        a = nw_ref[...] * (1.0 + sc_ref[0])
        h = (_rms_rows(x_ref[0]) * a + sh_ref[0]).astype(BF16)
        h_scr[...] = h
        gb_ref[0] = _dot(h, wgb_ref[...])

    acc = _dot(h_scr[...], w_ref[...])
    for j in range(tn // LANES):
        p_ref[0, j] = acc[:, j * LANES:(j + 1) * LANES].astype(p_ref.dtype)


def _inproj_raster(x, nw, sc, sh, w, wgb, col0, n, tm, tn):
    b, l, d = x.shape
    assert col0 % tn == 0 and n % tn == 0
    off = col0 // tn
    return pl.pallas_call(
        functools.partial(_inproj_raster_kernel, tn=tn),
        grid=(b, l // tm, n // tn),
        in_specs=[pl.BlockSpec((1, tm, d), lambda i, m, j: (i, m, 0)),
                  pl.BlockSpec((1, d), lambda i, m, j: (0, 0)),
                  pl.BlockSpec((1, 1, d), lambda i, m, j: (i, 0, 0)),
                  pl.BlockSpec((1, 1, d), lambda i, m, j: (i, 0, 0)),
                  pl.BlockSpec((d, tn), lambda i, m, j: (0, j + off)),
                  pl.BlockSpec((d, LANES), lambda i, m, j: (0, 0))],
        out_specs=[pl.BlockSpec((1, tn // LANES, tm, LANES), lambda i, m, j: (i, j, m, 0)),
                   pl.BlockSpec((1, tm, LANES), lambda i, m, j: (i, m, 0))],
        out_shape=[jax.ShapeDtypeStruct((b, n // LANES, l, LANES), BF16),
                   jax.ShapeDtypeStruct((b, l, LANES), F32)],
        scratch_shapes=[pltpu.VMEM((tm, d), BF16)],
        compiler_params=_cparams(("parallel", "parallel", "arbitrary")),
        name="inproj_raster",
    )(x, nw, sc, sh, w, wgb)


def _inproj_tb_kernel(x_ref, nw_ref, sc_ref, sh_ref, w_ref, o_ref, *, nbh, nb):
    bh = pl.program_id(1)
    per_b = x_ref.shape[1] * x_ref.shape[2]
    d = x_ref.shape[3]
    for i in range(nbh):
        a = nw_ref[...] * (1.0 + sc_ref[i])
        h = _rms_rows(x_ref[i].reshape(per_b, d)) * a + sh_ref[i]
        acc = _dot(h.astype(BF16), w_ref[...])
        for j in range(w_ref.shape[1] // LANES):
            o_ref[j, pl.ds(bh * nbh + i, per_b, stride=nb), :] = acc[:, j * LANES:(j + 1) * LANES]


def _inproj_tb(xv, nw, sc, sh, w, n, r, nbh):
    nb, r_total, c_total, d = xv.shape
    cw = SUBLANES
    assert nb % nbh == 0 and r_total % r == 0 and c_total % cw == 0 and (c_total == cw or r_total == r)
    per_b = r * cw
    n_m = (r_total // r) * (c_total // cw)
    x_map = (lambda m, bh: (bh, m, 0, 0)) if c_total == cw else (lambda m, bh: (bh, 0, m, 0))
    return pl.pallas_call(
        functools.partial(_inproj_tb_kernel, nbh=nbh, nb=nb),
        grid=(n_m, nb // nbh),
        in_specs=[pl.BlockSpec((nbh, r, cw, d), x_map),
                  pl.BlockSpec((1, d), lambda m, bh: (0, 0)),
                  pl.BlockSpec((nbh, 1, d), lambda m, bh: (bh, 0, 0)),
                  pl.BlockSpec((nbh, 1, d), lambda m, bh: (bh, 0, 0)),
                  pl.BlockSpec((d, n), lambda m, bh: (0, 0))],
        out_specs=pl.BlockSpec((n // LANES, per_b * nb, LANES), lambda m, bh: (0, m, 0)),
        out_shape=jax.ShapeDtypeStruct((n // LANES, n_m * per_b * nb, LANES), F32),
        compiler_params=_cparams(("parallel", "arbitrary")),
        name="inproj_tb",
    )(xv, nw, sc, sh, w)


def _lru_kernel(xc_ref, xl_ref, cw_ref, cb_ref, wg_ref, gbias_ref, lam_ref, o_ref,
                hsum, af, uf, ab, ub, *, lc, s, rows, nb):
    hd = LANES
    cpb = SUBLANES
    tseg = cpb * rows
    nblk = GRID_W // cpb
    cw = cw_ref[...]
    cb = cb_ref[...]
    half_nsp = (-0.5 * LRU_C) * _softplus(-lam_ref[...])

    def conv(prev, main, nxt):
        xm2 = jnp.concatenate([prev, main[:-2 * nb]], axis=0)
        xm1 = jnp.concatenate([prev[nb:], main[:-nb]], axis=0)
        xp1 = jnp.concatenate([main[nb:], nxt], axis=0)
        return cw[0:1] * xm2 + cw[1:2] * xm1 + cw[2:3] * main + cw[3:4] * xp1 + cb

    def ctx_xr(t0):
        ref = xc_ref.at[0]
        main = ref[pl.ds(pl.multiple_of(t0 * nb, SUBLANES), tseg * nb), :]
        p0 = jnp.maximum(t0 * nb - 2 * nb, 0)
        prev = ref[pl.ds(pl.multiple_of(p0, SUBLANES), 2 * nb), :]
        n0 = jnp.minimum((t0 + tseg) * nb, lc * nb - nb)
        nxt = ref[pl.ds(pl.multiple_of(n0, SUBLANES), nb), :]
        return conv(jnp.where(t0 > 0, prev, 0.0), main, jnp.where(t0 + tseg < lc, nxt, 0.0))

    def lat_xr(sg):
        main = jnp.concatenate([xl_ref[0, sg, :, w * nb:(w + 1) * nb, :].reshape(rows * nb, hd)
                                for w in range(cpb)], axis=0)
        prev = xl_ref[0, jnp.maximum(sg - 1, 0), rows - 2:rows, (cpb - 1) * nb:cpb * nb, :].reshape(2 * nb, hd)
        nxt = xl_ref[0, jnp.minimum(sg + 1, nblk - 1), 0:1, 0:nb, :].reshape(nb, hd)
        return conv(jnp.where(sg > 0, prev, 0.0), main, jnp.where(sg < nblk - 1, nxt, 0.0))

    def gates(xr, direction, a_ref, u_ref):
        lo = direction * 2 * hd
        th_g = jnp.tanh(_dot(xr.astype(BF16), wg_ref[0, :, lo:lo + 2 * hd]) + gbias_ref[0, :, lo:lo + 2 * hd])
        hnsp = half_nsp[:, direction * hd:(direction + 1) * hd]
        log_a = th_g[:, :hd] * hnsp + hnsp
        ig = 0.5 * th_g[:, hd:] + 0.5
        a = jnp.exp(log_a)
        a_ref[...] = a
        sq = -jnp.tanh(log_a) * (1.0 + a * a)
        root = jnp.where(sq > 0.0, sq * lax.rsqrt(sq), 0.0)
        u_ref[...] = root * (ig * xr)

    def scan_segment(hf, hb, store, fseg, bseg):
        for blk in range(tseg // LRU_SCAN_BLOCK):
            for direction in range(N_DIR):
                a_ref, u_ref = (af, uf) if direction == 0 else (ab, ub)
                h0 = hf if direction == 0 else hb
                sg = fseg if direction == 0 else bseg
                pa = pu = None
                for kk in range(LRU_SCAN_BLOCK):
                    step = blk * LRU_SCAN_BLOCK + kk
                    t = step if direction == 0 else tseg - 1 - step
                    a = a_ref[t * nb:(t + 1) * nb, :]
                    u = u_ref[t * nb:(t + 1) * nb, :]
                    pa, pu = (a, u) if kk == 0 else (a * pa, a * pu + u)
                    h = pa * h0 + pu
                    w, r = t // rows, t % rows
                    if store == "set":
                        hsum[sg, r, w * nb:(w + 1) * nb, :] = h
                    elif store == "add":
                        hsum[sg, r, w * nb:(w + 1) * nb, :] = hsum[sg, r, w * nb:(w + 1) * nb, :] + h
                if direction == 0:
                    hf = h
                else:
                    hb = h
        return hf, hb

    h0 = jnp.zeros((nb, hd), F32)

    nseg_c = lc // tseg
    def ctx_body(i, carry):
        gates(ctx_xr(i * tseg), 0, af, uf)
        gates(ctx_xr((nseg_c - 1 - i) * tseg), 1, ab, ub)
        return scan_segment(carry[0], carry[1], None, 0, 0)
    hf, hb = lax.fori_loop(0, nseg_c, ctx_body, (h0, h0))

    def lat_body(store):
        def body(i, carry):
            bseg = nblk - 1 - i
            gates(lat_xr(i), 0, af, uf)
            gates(lat_xr(bseg), 1, ab, ub)
            return scan_segment(carry[0], carry[1], store, i, bseg)
        return body
    hf, hb = lax.fori_loop(0, nblk // 2, lat_body("set"), (hf, hb))
    lax.fori_loop(nblk // 2, nblk, lat_body("add"), (hf, hb))

    def out_body(rr, carry):
        for i in range(nb):
            for wg in range(nblk):
                o_ref[i, pl.ds(pl.multiple_of(rr * GRID_W + wg * cpb, SUBLANES), cpb), :] = (
                    hsum[wg, rr, pl.ds(i, cpb, stride=nb), :])
        return carry
    lax.fori_loop(0, rows, out_body, 0)


def _lru(xc, xl, conv_w, conv_b, wg, gbias, lam2, nb, lc, s):
    heads = xl.shape[0]
    rows = s // GRID_W
    tseg = SUBLANES * rows
    nblk = GRID_W // SUBLANES
    assert lc % tseg == 0 and nblk % 2 == 0 and tseg % LRU_SCAN_BLOCK == 0 and nb == SUBLANES
    return pl.pallas_call(
        functools.partial(_lru_kernel, lc=lc, s=s, rows=rows, nb=nb),
        grid=(heads,),
        in_specs=[pl.BlockSpec((1, lc * nb, LANES), lambda h: (h, 0, 0)),
                  pl.BlockSpec((1, nblk, rows, SUBLANES * nb, LANES), lambda h: (h, 0, 0, 0, 0)),
                  pl.BlockSpec((4, LANES), lambda h: (0, h)),
                  pl.BlockSpec((1, LANES), lambda h: (0, h)),
                  pl.BlockSpec((1, LANES, 4 * LANES), lambda h: (h, 0, 0)),
                  pl.BlockSpec((1, 1, 4 * LANES), lambda h: (h, 0, 0)),
                  pl.BlockSpec((1, 2 * LANES), lambda h: (0, h))],
        out_specs=pl.BlockSpec((nb, s, LANES), lambda h: (0, 0, h)),
        out_shape=jax.ShapeDtypeStruct((nb, s, heads * LANES), F32),
        scratch_shapes=[pltpu.VMEM((nblk, rows, SUBLANES * nb, LANES), F32)]
                       + [pltpu.VMEM((tseg * nb, LANES), F32)] * 4,
        compiler_params=_cparams(("parallel",)),
        name="lru",
    )(xc, xl, conv_w, conv_b, wg, gbias, lam2)


def _bmm(a, b):
    return lax.dot_general(a, b, (((2,), (1,)), ((0,), (0,))), preferred_element_type=F32)


def _bmm_nt(a, b):
    return lax.dot_general(a, b, (((2,), (2,)), ((0,), (0,))), preferred_element_type=F32)


def _inv_unit_triangular_x4(lm):
    n, c, w = lm.shape
    nblk = w // c
    ri = lax.broadcasted_iota(jnp.int32, (w, w), 0)
    ci = lax.broadcasted_iota(jnp.int32, (w, w), 1)
    on_diag_block = (ri // c) == (ci // c)
    eye = (lax.broadcasted_iota(jnp.int32, (c, w), 0)
           == lax.broadcasted_iota(jnp.int32, (c, w), 1) % c).astype(F32)

    def block_diag(x):
        return jnp.where(on_diag_block, jnp.concatenate([x] * nblk, axis=1), 0.0).astype(BF16)

    p = eye - lm
    lk = _bmm(lm.astype(BF16), block_diag(lm))
    n_sq = int(math.log2(c)) - 2
    for _ in range(n_sq):
        x = _bmm(jnp.concatenate([p, lk], axis=1).astype(BF16), block_diag(lk))
        p = p + x[:, :c]
        lk = x[:, c:]
    return p + _bmm(p.astype(BF16), block_diag(lk))


def _gdn_kernel(qc_ref, kc_ref, vc_ref, ql_ref, kl_ref, vl_ref, z_ref, g_ref, alog_ref, dtb_ref,
                cwq_ref, cwk_ref, cwv_ref, nw_ref, o_ref,
                qs, ks, vs, qcol, grow, grow2, xpad, aq_s, b_s, op_s, st_s, o_s, *, lc, s, hg, nbc):
    dk = LANES
    c = CHUNK
    l_all = lc + s
    ncc = lc // c
    ncl = s // c
    nct = ncc + ncl
    npb = nbc // 2

    ii = lax.broadcasted_iota(jnp.int32, (c, 2 * c), 0)
    jl = lax.broadcasted_iota(jnp.int32, (c, 2 * c), 1)
    fwd_half = jl < c
    jj = jnp.where(fwd_half, jl, jl - c)
    keep2 = (fwd_half & (ii >= jj)) | ((jl >= c) & (ii <= jj))
    strict2 = (fwd_half & (ii > jj)) | ((jl >= c) & (ii < jj))
    lane1 = lax.broadcasted_iota(jnp.int32, (1, LANES), 1)
    eye_k = lax.broadcasted_iota(jnp.int32, (dk, dk), 0) == lax.broadcasted_iota(jnp.int32, (dk, dk), 1)
    blk = 2 * LANES
    si = lax.broadcasted_iota(jnp.int32, (blk, blk), 0)
    ji = lax.broadcasted_iota(jnp.int32, (blk, blk), 1)
    same = (si // c) == (ji // c)
    t_pre = jnp.where(same & (si <= ji), 1.0, 0.0).astype(F32)
    t_suf = jnp.where(same & (si >= ji), 1.0, 0.0).astype(F32)

    def conv_silu(ref, hd, cw, n):
        p = SUBLANES
        zero = jnp.zeros((p, dk), F32)
        xpad[0:p] = zero
        xpad[p + n:2 * p + n] = zero
        xpad[p:p + n] = ref[0, hd].astype(F32)
        return _silu(cw[0:1] * xpad[p - 2:p - 2 + n] + cw[1:2] * xpad[p - 1:p - 1 + n]
                     + cw[2:3] * xpad[p:p + n] + cw[3:4] * xpad[p + 1:p + 1 + n])

    def l2n(t):
        return t * lax.rsqrt(jnp.sum(t * t, axis=-1, keepdims=True) + NORM_EPS)

    for hd in range(hg):
        cwq = cwq_ref[:, hd * dk:(hd + 1) * dk]
        cwk = cwk_ref[:, hd * dk:(hd + 1) * dk]
        cwv = cwv_ref[:, hd * dk:(hd + 1) * dk]
        qs[0:lc] = l2n(conv_silu(qc_ref, hd, cwq, lc)) * (dk ** -0.5)
        qs[lc:l_all] = l2n(conv_silu(ql_ref, hd, cwq, s)) * (dk ** -0.5)
        ks[0:lc] = l2n(conv_silu(kc_ref, hd, cwk, lc))
        ks[lc:l_all] = l2n(conv_silu(kl_ref, hd, cwk, s))
        vs[0:lc] = conv_silu(vc_ref, hd, cwv, lc)
        vs[lc:l_all] = conv_silu(vl_ref, hd, cwv, s)

        r = g_ref[0, hd]
        rid8 = lax.broadcasted_iota(jnp.int32, (SUBLANES, l_all), 0)
        aneg = -jnp.exp(alog_ref[hd][:, 0:1])
        dtb = dtb_ref[hd][:, 0:1]
        val = jnp.where(rid8 < 2, _sigmoid(r), aneg * _softplus(r + dtb))
        for i in range(l_all // blk):
            vb = val[:, i * blk:(i + 1) * blk]
            pre = jnp.dot(vb, t_pre, preferred_element_type=F32, precision=lax.Precision.HIGHEST)
            suf = jnp.dot(vb, t_suf, preferred_element_type=F32, precision=lax.Precision.HIGHEST)
            rb = lax.broadcasted_iota(jnp.int32, (SUBLANES, blk), 0)
            gblk = jnp.where(rb == 2, pre, jnp.where(rb == 3, suf, vb))
            grow[:, i * blk:(i + 1) * blk] = gblk
            for p2 in range(blk // LANES):
                g128 = gblk[:, p2 * LANES:(p2 + 1) * LANES]
                rolled = pltpu.roll(g128, c, 1)
                ch = (i * (blk // LANES) + p2) * 2
                grow2[0:1, ch * LANES:(ch + 1) * LANES] = jnp.where(lane1 < c, g128[2:3], rolled[3:4])
                grow2[0:1, (ch + 1) * LANES:(ch + 2) * LANES] = jnp.where(lane1 < c, rolled[2:3], g128[3:4])
        gfull = jnp.concatenate([grow[...], jnp.zeros((LANES - SUBLANES, l_all), F32)], axis=0)
        qcol[...] = gfull.T

        def intra(it, carry, hd=hd):
            r0 = pl.multiple_of(it * (nbc * c), 2 * c)
            q = qs[pl.ds(r0, nbc * c), :].reshape(nbc, c, dk)
            k = ks[pl.ds(r0, nbc * c), :].reshape(nbc, c, dk)
            v = vs[pl.ds(r0, nbc * c), :].reshape(nbc, c, dk)
            col = qcol[pl.ds(r0, nbc * c), :].reshape(nbc, c, LANES)
            g2 = grow2[0:1, pl.ds(pl.multiple_of(it * (nbc * LANES), LANES), nbc * LANES)]
            grw2 = jnp.stack([g2[:, j * LANES:(j + 1) * LANES] for j in range(nbc)], axis=0)
            beta = [col[:, :, d:d + 1] for d in range(N_DIR)]
            gcol = [col[:, :, 2 + d:3 + d] for d in range(N_DIR)]
            beta2 = jnp.where(fwd_half, beta[0], beta[1])
            gcol2 = jnp.where(fwd_half, gcol[0], gcol[1])
            kb16 = k.astype(BF16)
            both = _bmm_nt(jnp.concatenate([q.astype(BF16), kb16], axis=1),
                           jnp.concatenate([kb16, kb16], axis=1))
            decay2 = jnp.exp(jnp.where(keep2, gcol2 - grw2, NEG_BIG))
            lm2 = jnp.where(strict2, beta2 * both[:, c:] * decay2, 0.0)
            qkd2 = (both[:, :c] * decay2).astype(BF16)
            half = nbc // 2
            tinv4 = _inv_unit_triangular_x4(jnp.concatenate([lm2[:half], lm2[half:]], axis=2))
            tinv2 = jnp.concatenate([tinv4[:, :, :2 * c], tinv4[:, :, 2 * c:]], axis=0)
            eg = [jnp.exp(g) for g in gcol]
            glast = [gcol[0][:, c - 1:c, :], gcol[1][:, 0:1, :]]
            rhs = [jnp.concatenate([v * beta[d], (k * beta[d]) * eg[d]], axis=2).astype(BF16) for d in range(N_DIR)]
            zero = jnp.zeros((nbc, c, 2 * dk), BF16)
            rhs_bd = jnp.concatenate([jnp.concatenate([rhs[0], zero], axis=2),
                                      jnp.concatenate([zero, rhs[1]], axis=2)], axis=1)
            uw2 = _bmm(tinv2.astype(BF16), rhs_bd).astype(BF16)
            uw_bd = jnp.concatenate([jnp.concatenate([uw2[:, :, :2 * dk], zero], axis=2),
                                     jnp.concatenate([zero, uw2[:, :, 2 * dk:]], axis=2)], axis=1)
            kd = jnp.concatenate([k * jnp.exp(glast[d] - gcol[d]) for d in range(N_DIR)], axis=1)
            kdt2 = jnp.swapaxes(kd, 1, 2).astype(BF16)
            m2 = _bmm(jnp.concatenate([kdt2, qkd2], axis=1), uw_bd)
            def split_pairs(t):
                t = t.reshape((npb, 2) + t.shape[1:])
                return t[:, 0], t[:, 1]
            for d in range(N_DIR):
                m = m2[:, :, d * 2 * dk:(d + 1) * 2 * dk]
                a_mat = jnp.where(eye_k, jnp.exp(glast[d]), 0.0) - m[:, :dk, dk:]
                q_mat = q * eg[d] - m[:, dk:, dk:]
                parts = [split_pairs(t) for t in (a_mat, m[:, :dk, :dk], q_mat, m[:, dk:, :dk])]
                (a1, b1, q1, o1), (a2, b2, q2, o2) = [[p[first ^ d] for p in parts] for first in (0, 1)]
                mm = _bmm(jnp.concatenate([a2, q2], axis=1).astype(BF16),
                          jnp.concatenate([a1, b1], axis=2).astype(BF16))
                q21 = mm[:, dk:, :dk]
                o21 = mm[:, dk:, dk:] + o2
                q_tok = (q1, q21) if d == 0 else (q21, q1)
                o_tok = (o1, o21) if d == 0 else (o21, o1)
                aq_s[hd, d, pl.ds(it * npb, npb)] = jnp.concatenate((mm[:, :dk, :dk],) + q_tok, axis=1).astype(BF16)
                b_s[hd, d, pl.ds(it * npb, npb)] = (mm[:, :dk, dk:] + b2).astype(BF16)
                op_s[hd, d, pl.ds(it * npb, npb)] = jnp.concatenate(o_tok, axis=1).astype(BF16)
            return carry
        for it in range(nct // nbc):
            intra(it, 0)

    st_s[...] = jnp.zeros_like(st_s)
    o_s[...] = jnp.zeros_like(o_s)
    ncp = ncc // 2
    nlp = ncl // 2

    def recur(t, carry):
        pf = t
        pb = jnp.where(t < ncp, ncp - 1 - t, 2 * ncp + nlp - 1 - t)
        chains = [(hd, d, pi) for hd in range(hg) for d, pi in ((0, pf), (1, pb))]
        loaded = []
        for hd, d, pi in chains:
            ro = pl.multiple_of(jnp.where(t >= ncp, pi * (2 * c) - lc, s), 2 * c)
            loaded.append((aq_s[hd, d, pi], st_s[hd, d], b_s[hd, d, pi], op_s[hd, d, pi],
                           o_s[hd, pl.ds(ro, 2 * c), :], ro))
        results = []
        for aq, st, bm, om, o_old, ro in loaded:
            x = _dot(aq, st.astype(BF16))
            results.append((x[:dk] + bm.astype(F32), o_old + x[dk:] + om.astype(F32), ro))
        for (hd, d, pi), (st_new, o_new, ro) in zip(chains, results):
            st_s[hd, d] = st_new
            o_s[hd, pl.ds(ro, 2 * c), :] = o_new
        return carry
    lax.fori_loop(0, ncp + nlp, recur, 0)

    for hd in range(hg):
        z = z_ref[0, hd].astype(F32)
        o_ref[0, :, hd * dk:(hd + 1) * dk] = (
            (_rms_rows(o_s[hd, 0:s]) * nw_ref[...]).astype(F32) * _silu(z)).astype(o_ref.dtype)


def _gdn(pc, plat, grow, alog8, dtb8, conv_w, norm_w, lc, s):
    b = plat.shape[0]
    heads = GDN_HEADS
    l_all = lc + s
    nct = l_all // CHUNK
    assert lc % CHUNK == 0 and s % CHUNK == 0 and l_all % (2 * LANES) == 0
    assert GDN_CHUNK_BATCH % 2 == 0 and nct % GDN_CHUNK_BATCH == 0
    assert lc % (2 * CHUNK) == 0 and s % (4 * CHUNK) == 0
    hg = GDN_HEADS_PER_STEP
    assert heads % hg == 0
    def slot(base):
        return lambda i, h: (i, base // hg + h, 0, 0)
    def cw(base):
        return lambda i, h: (0, base // hg + h)
    return pl.pallas_call(
        functools.partial(_gdn_kernel, lc=lc, s=s, hg=hg, nbc=GDN_CHUNK_BATCH),
        grid=(b, heads // hg),
        in_specs=[pl.BlockSpec((1, hg, lc, LANES), slot(0)),
                  pl.BlockSpec((1, hg, lc, LANES), slot(heads)),
                  pl.BlockSpec((1, hg, lc, LANES), slot(2 * heads)),
                  pl.BlockSpec((1, hg, s, LANES), slot(heads)),
                  pl.BlockSpec((1, hg, s, LANES), slot(2 * heads)),
                  pl.BlockSpec((1, hg, s, LANES), slot(3 * heads)),
                  pl.BlockSpec((1, hg, s, LANES), slot(4 * heads)),
                  pl.BlockSpec((1, hg, SUBLANES, l_all), lambda i, h: (i, h, 0, 0)),
                  pl.BlockSpec((hg, SUBLANES, LANES), lambda i, h: (h, 0, 0)),
                  pl.BlockSpec((hg, SUBLANES, LANES), lambda i, h: (h, 0, 0)),
                  pl.BlockSpec((4, hg * LANES), cw(0)),
                  pl.BlockSpec((4, hg * LANES), cw(heads)),
                  pl.BlockSpec((4, hg * LANES), cw(2 * heads)),
                  pl.BlockSpec((1, LANES), lambda i, h: (0, 0))],
        out_specs=pl.BlockSpec((1, s, hg * LANES), lambda i, h: (i, 0, h)),
        out_shape=jax.ShapeDtypeStruct((b, s, heads * LANES), BF16),
        scratch_shapes=[pltpu.VMEM((l_all, LANES), F32),
                        pltpu.VMEM((l_all, LANES), F32),
                        pltpu.VMEM((l_all, LANES), F32),
                        pltpu.VMEM((l_all, LANES), F32),
                        pltpu.VMEM((SUBLANES, l_all), F32),
                        pltpu.VMEM((SUBLANES, nct * LANES), F32),
                        pltpu.VMEM((s + 2 * SUBLANES, LANES), F32),
                        pltpu.VMEM((hg, N_DIR, nct // 2, LANES + 2 * CHUNK, LANES), BF16),
                        pltpu.VMEM((hg, N_DIR, nct // 2, LANES, LANES), BF16),
                        pltpu.VMEM((hg, N_DIR, nct // 2, 2 * CHUNK, LANES), BF16),
                        pltpu.VMEM((hg, N_DIR, LANES, LANES), F32),
                        pltpu.VMEM((hg, s + 2 * CHUNK, LANES), F32)],
        compiler_params=_cparams(("parallel", "arbitrary")),
        name="gdn",
    )(pc, pc, pc, plat, plat, plat, plat, grow, alog8, dtb8, conv_w, conv_w, conv_w, norm_w)


def _out_kernel(hs_ref, y_ref, gdn_ref, w_ref, x_ref, gm_ref, scf_ref, shf_ref, nw1_ref, nw2_ref,
                x1_ref, h2_ref, *, heads):
    half = heads * LANES
    a2 = nw2_ref[...] * (1.0 + scf_ref[0])
    tm = x_ref.shape[1]
    for r0 in range(0, tm, OUT_SUB_ROWS):
        rs = slice(r0, r0 + OUT_SUB_ROWS)
        y = jnp.concatenate([y_ref[0, j, rs, :] for j in range(heads)], axis=-1).astype(F32)
        lru = (hs_ref[0, rs, :] * _gelu_tanh(y)).astype(BF16)
        m = _dot(lru, w_ref[0:half, :]) + _dot(gdn_ref[0, rs, :], w_ref[half:, :])
        x1 = x_ref[0, rs, :] + gm_ref[0] * (_rms_rows(m) * nw1_ref[...])
        x1_ref[0, rs, :] = x1
        h2_ref[0, rs, :] = (_rms_rows(x1) * a2 + shf_ref[0]).astype(BF16)


def _out_proj(hs, plat, gdn, w_out, x, gm, scf, shf, nw1, nw2, tm):
    b, s, d = x.shape
    heads = LRU_HEADS
    dm = w_out.shape[0]
    row = lambda i, m: (i, m, 0)
    vec = lambda i, m: (i, 0, 0)
    fix = lambda i, m: (0, 0)
    return pl.pallas_call(
        functools.partial(_out_kernel, heads=heads),
        grid=(b, s // tm),
        in_specs=[pl.BlockSpec((1, tm, heads * LANES), row),
                  pl.BlockSpec((1, heads, tm, LANES), lambda i, m: (i, 0, m, 0)),
                  pl.BlockSpec((1, tm, dm - heads * LANES), row),
                  pl.BlockSpec((dm, d), fix),
                  pl.BlockSpec((1, tm, d), row),
                  pl.BlockSpec((1, 1, d), vec),
                  pl.BlockSpec((1, 1, d), vec),
                  pl.BlockSpec((1, 1, d), vec),
                  pl.BlockSpec((1, d), fix),
                  pl.BlockSpec((1, d), fix)],
        out_specs=[pl.BlockSpec((1, tm, d), row), pl.BlockSpec((1, tm, d), row)],
        out_shape=[jax.ShapeDtypeStruct((b, s, d), F32), jax.ShapeDtypeStruct((b, s, d), BF16)],
        compiler_params=_cparams(("parallel", "parallel")),
        name="out_proj",
    )(hs, plat, gdn, w_out, x, gm, scf, shf, nw1, nw2)


def _mlp_kernel(h_ref, w1_ref, w2_ref, x1_ref, gf_ref, nw_ref, o_ref):
    f = pl.program_id(2)

    @pl.when(f == 0)
    def _():
        o_ref[0] = jnp.zeros_like(o_ref[0])

    hid = jnp.maximum(_dot(h_ref[0], w1_ref[...]), 0.0)
    o_ref[0] = o_ref[0] + _dot((hid * hid).astype(BF16), w2_ref[...])

    @pl.when(f == pl.num_programs(2) - 1)
    def _():
        o_ref[0] = x1_ref[0] + gf_ref[0] * (_rms_rows(o_ref[0]) * nw_ref[...])


def _mlp(h2, w1, w2, x1, gf, nw3, tm, tf):
    b, s, d = x1.shape
    ff = w1.shape[1]
    row = lambda i, m, f: (i, m, 0)
    return pl.pallas_call(
        _mlp_kernel,
        grid=(b, s // tm, ff // tf),
        in_specs=[pl.BlockSpec((1, tm, d), row),
                  pl.BlockSpec((d, tf), lambda i, m, f: (0, f)),
                  pl.BlockSpec((tf, d), lambda i, m, f: (f, 0)),
                  pl.BlockSpec((1, tm, d), row),
                  pl.BlockSpec((1, 1, d), lambda i, m, f: (i, 0, 0)),
                  pl.BlockSpec((1, d), lambda i, m, f: (0, 0))],
        out_specs=pl.BlockSpec((1, tm, d), row),
        out_shape=jax.ShapeDtypeStruct((b, s, d), F32),
        compiler_params=_cparams(("parallel", "parallel", "arbitrary"), VMEM_LIMIT_MLP),
        name="mlp",
    )(h2, w1, w2, x1, gf, nw3)


def kernel(x, c, ctx, c_ctx, w_mod, b_mod, norm_w, w_in, lru_conv_w, lru_conv_b, lru_gate_w, lru_gate_b,
           lru_lambda, gdn_conv_w, gdn_a_log, gdn_dt_bias, gdn_norm_w, w_out, w_ff1, w_ff2):
    b, s, d = x.shape
    lc = ctx.shape[1]
    l_all = lc + s
    rows = s // GRID_W
    d_lru = LRU_HEADS * LANES
    d_gdn = GDN_HEADS * LANES
    assert w_mod.shape[0] == 1 and b == SUBLANES and d == d_lru + d_gdn

    cc = jnp.concatenate([c, c_ctx[None], jnp.zeros((2 * SUBLANES - b - 1, d), F32)], axis=0)
    mod = _modulation(cc, w_mod[0], b_mod[0])
    sh_m, sc_m, g_m, sh_f, sc_f, g_f = [mod[:b, i * d:(i + 1) * d] for i in range(6)]
    csh_m = jnp.broadcast_to(mod[b:b + 1, 0:d], (b, d))
    csc_m = jnp.broadcast_to(mod[b:b + 1, d:2 * d], (b, d))
    nw = norm_w[0]
    nw0, nw1, nw2, nw3 = [nw[i:i + 1] for i in range(4)]
    v3 = lambda t: t.reshape(b, 1, d)

    w_in0 = w_in[0]
    y_end = 2 * d_lru
    qkv_end = y_end + 3 * d_gdn
    z_end = qkv_end + d_gdn
    w_bf = w_in0.astype(BF16)
    n_gb = w_in0.shape[1] - z_end
    w_gb = jnp.pad(w_in0[:, z_end:], ((0, 0), (0, LANES - n_gb))).astype(BF16)

    p_lat, gb_lat = _inproj_raster(x, nw0, v3(sc_m), v3(sh_m), w_bf, w_gb, col0=d_lru, n=z_end - d_lru,
                                   tm=1024, tn=1024)
    p_ctx, gb_ctx = _inproj_raster(ctx, nw0, v3(csc_m), v3(csh_m), w_bf, w_gb, col0=y_end, n=qkv_end - y_end,
                                   tm=lc, tn=1024)
    xl = _inproj_tb(x.reshape(b, rows, GRID_W, d), nw0, v3(sc_m), v3(sh_m), w_bf, n=d_lru, r=rows, nbh=b // 2)
    xl = xl.reshape(LRU_HEADS, GRID_W // SUBLANES, rows, SUBLANES * b, LANES)
    xc = _inproj_tb(ctx.reshape(b, lc // SUBLANES, SUBLANES, d), nw0, v3(csc_m), v3(csh_m), w_bf, n=d_lru,
                    r=2 * SUBLANES, nbh=b)

    gw = lru_gate_w[0]
    wg = (0.5 * jnp.transpose(gw, (2, 3, 0, 1, 4))).reshape(LRU_HEADS, LANES, 4 * LANES).astype(BF16)
    gbias = 0.5 * jnp.transpose(lru_gate_b[0].reshape(N_DIR, 2, LRU_HEADS, LANES), (2, 0, 1, 3))
    gbias = gbias.reshape(LRU_HEADS, 1, 4 * LANES)
    lam2 = jnp.transpose(lru_lambda[0].reshape(N_DIR, LRU_HEADS, LANES), (1, 0, 2)).reshape(1, 2 * d_lru)
    hs = _lru(xc, xl, lru_conv_w[0], lru_conv_b[0].reshape(1, d_lru), wg, gbias, lam2, b, lc, s)

    gb = jnp.concatenate([gb_ctx, gb_lat], axis=1)[:, :, :n_gb]
    gb = gb.reshape(b, l_all, 2, N_DIR, GDN_HEADS)
    grow = jnp.transpose(gb, (0, 4, 2, 3, 1)).reshape(b, GDN_HEADS, 2 * N_DIR, l_all)
    grow = jnp.pad(grow, ((0, 0), (0, 0), (0, SUBLANES - 2 * N_DIR), (0, 0)))
    def rows8(t):
        t = jnp.transpose(t, (1, 0))[:, :, None]
        t = jnp.pad(t, ((0, 0), (2, SUBLANES - 2 - N_DIR), (0, 0)))
        return jnp.broadcast_to(t, (GDN_HEADS, SUBLANES, LANES)).astype(F32)
    gdn = _gdn(p_ctx, p_lat, grow, rows8(gdn_a_log[0]), rows8(gdn_dt_bias[0]), gdn_conv_w[0],
               gdn_norm_w[0].reshape(1, LANES), lc, s)

    x1, h2 = _out_proj(hs, p_lat, gdn, w_out[0].astype(BF16), x, v3(g_m), v3(sc_f), v3(sh_f),
                       nw1, nw2, tm=512)
    return _mlp(h2, w_ff1[0].astype(BF16), w_ff2[0].astype(BF16), x1, v3(g_f), nw3, tm=512, tf=2048)
```

```python
import functools
import math

import jax
import jax.numpy as jnp
from jax import lax
from jax.experimental import pallas as pl
from jax.experimental.pallas import tpu as pltpu

F32 = jnp.float32
BF16 = jnp.bfloat16

LANES = 128
SUBLANES = 8
VMEM_LIMIT = 56 * 1024 * 1024
VMEM_LIMIT_MLP = 62 * 1024 * 1024

GRID_W = 64
NORM_EPS = 1e-6
LRU_C = 8.0
LRU_HEADS = 8
GDN_HEADS = 8
N_DIR = 2
CHUNK = 64
NEG_BIG = -1e30
OUT_SUB_ROWS = 256
LRU_SCAN_BLOCK = 8
GDN_HEADS_PER_STEP = 2
GDN_CHUNK_BATCH = 18


def _cparams(sem, vmem_limit=VMEM_LIMIT):
    return pltpu.CompilerParams(dimension_semantics=sem, vmem_limit_bytes=vmem_limit)


def _dot(a, b):
    return jnp.dot(a, b, preferred_element_type=F32)


def _dot_nt(a, b):
    return lax.dot_general(a, b, (((1,), (1,)), ((), ())), preferred_element_type=F32)


def _sigmoid(t):
    return 0.5 * jnp.tanh(0.5 * t) + 0.5


def _softplus(t):
    return jnp.maximum(t, 0.0) + jnp.log(1.0 + jnp.exp(-jnp.abs(t)))


def _silu(t):
    return t * _sigmoid(t)


def _gelu_tanh(t):
    return 0.5 * t * (1.0 + jnp.tanh(math.sqrt(2.0 / math.pi) * (t + 0.044715 * (t * t * t))))


def _rms_rows(t):
    return t * lax.rsqrt(jnp.mean(t * t, axis=-1, keepdims=True) + NORM_EPS)


def _mod_kernel(c_ref, w_ref, b_ref, o_ref):
    s = _silu(c_ref[...])
    o_ref[...] = _dot(s.astype(BF16), w_ref[...].astype(BF16)) + b_ref[...]


def _modulation(cc, w_mod, b_mod):
    rows, d = cc.shape
    n = w_mod.shape[1]
    tn = 1024
    return pl.pallas_call(
        _mod_kernel,
        grid=(n // tn,),
        in_specs=[pl.BlockSpec((rows, d), lambda j: (0, 0)),
                  pl.BlockSpec((d, tn), lambda j: (0, j)),
                  pl.BlockSpec((1, tn), lambda j: (0, j))],
        out_specs=pl.BlockSpec((rows, tn), lambda j: (0, j)),
        out_shape=jax.ShapeDtypeStruct((rows, n), F32),
        compiler_params=_cparams(("arbitrary",)),
        name="mod",
    )(cc, w_mod, b_mod.reshape(1, n))


def _inproj_raster_kernel(x_ref, nw_ref, sc_ref, sh_ref, w_ref, wgb_ref, p_ref, gb_ref, h_scr, *, tn):
    @pl.when(pl.program_id(2) == 0)
    def _():
        a = nw_ref[...] * (1.0 + sc_ref[0])
        h = (_rms_rows(x_ref[0]) * a + sh_ref[0]).astype(BF16)
        h_scr[...] = h
        gb_ref[0] = _dot(h, wgb_ref[...])

    acc = _dot(h_scr[...], w_ref[...].astype(BF16))
    for j in range(tn // LANES):
        p_ref[0, j] = acc[:, j * LANES:(j + 1) * LANES].astype(p_ref.dtype)


def _inproj_raster(x, nw, sc, sh, w, wgb, col0, n, tm, tn):
    b, l, d = x.shape
    assert col0 % tn == 0 and n % tn == 0
    off = col0 // tn
    return pl.pallas_call(
        functools.partial(_inproj_raster_kernel, tn=tn),
        grid=(b, l // tm, n // tn),
        in_specs=[pl.BlockSpec((1, tm, d), lambda i, m, j: (i, m, 0)),
                  pl.BlockSpec((1, d), lambda i, m, j: (0, 0)),
                  pl.BlockSpec((1, 1, d), lambda i, m, j: (i, 0, 0)),
                  pl.BlockSpec((1, 1, d), lambda i, m, j: (i, 0, 0)),
                  pl.BlockSpec((d, tn), lambda i, m, j: (0, j + off)),
                  pl.BlockSpec((d, LANES), lambda i, m, j: (0, 0))],
        out_specs=[pl.BlockSpec((1, tn // LANES, tm, LANES), lambda i, m, j: (i, j, m, 0)),
                   pl.BlockSpec((1, tm, LANES), lambda i, m, j: (i, m, 0))],
        out_shape=[jax.ShapeDtypeStruct((b, n // LANES, l, LANES), BF16),
                   jax.ShapeDtypeStruct((b, l, LANES), F32)],
        scratch_shapes=[pltpu.VMEM((tm, d), BF16)],
        compiler_params=_cparams(("parallel", "parallel", "arbitrary")),
        name="inproj_raster",
    )(x, nw, sc, sh, w, wgb)


def _inproj_tb_kernel(x_ref, nw_ref, sc_ref, sh_ref, w_ref, o_ref, *, nbh, nb):
    bh = pl.program_id(1)
    per_b = x_ref.shape[1] * x_ref.shape[2]
    d = x_ref.shape[3]
    wb = w_ref[...].astype(BF16)
    for i in range(nbh):
        a = nw_ref[...] * (1.0 + sc_ref[i])
        h = _rms_rows(x_ref[i].reshape(per_b, d)) * a + sh_ref[i]
        acc = _dot(h.astype(BF16), wb)
        for j in range(w_ref.shape[1] // LANES):
            o_ref[j, pl.ds(bh * nbh + i, per_b, stride=nb), :] = acc[:, j * LANES:(j + 1) * LANES]


def _inproj_tb(xv, nw, sc, sh, w, n, r, nbh):
    nb, r_total, c_total, d = xv.shape
    cw = SUBLANES
    assert nb % nbh == 0 and r_total % r == 0 and c_total % cw == 0 and (c_total == cw or r_total == r)
    per_b = r * cw
    n_m = (r_total // r) * (c_total // cw)
    x_map = (lambda m, bh: (bh, m, 0, 0)) if c_total == cw else (lambda m, bh: (bh, 0, m, 0))
    return pl.pallas_call(
        functools.partial(_inproj_tb_kernel, nbh=nbh, nb=nb),
        grid=(n_m, nb // nbh),
        in_specs=[pl.BlockSpec((nbh, r, cw, d), x_map),
                  pl.BlockSpec((1, d), lambda m, bh: (0, 0)),
                  pl.BlockSpec((nbh, 1, d), lambda m, bh: (bh, 0, 0)),
                  pl.BlockSpec((nbh, 1, d), lambda m, bh: (bh, 0, 0)),
                  pl.BlockSpec((d, n), lambda m, bh: (0, 0))],
        out_specs=pl.BlockSpec((n // LANES, per_b * nb, LANES), lambda m, bh: (0, m, 0)),
        out_shape=jax.ShapeDtypeStruct((n // LANES, n_m * per_b * nb, LANES), F32),
        compiler_params=_cparams(("parallel", "arbitrary")),
        name="inproj_tb",
    )(xv, nw, sc, sh, w)


def _lru_kernel(xc_ref, xl_ref, cw_ref, cb_ref, wg_ref, gbias_ref, lam_ref, o_ref,
                hsum, af, uf, ab, ub, *, lc, s, rows, nb):
    hd = LANES
    cpb = SUBLANES
    tseg = cpb * rows
    nblk = GRID_W // cpb
    cw = cw_ref[...]
    cb = cb_ref[...]
    half_nsp = (-0.5 * LRU_C) * _softplus(-lam_ref[...])

    def conv(prev, main, nxt):
        xm2 = jnp.concatenate([prev, main[:-2 * nb]], axis=0)
        xm1 = jnp.concatenate([prev[nb:], main[:-nb]], axis=0)
        xp1 = jnp.concatenate([main[nb:], nxt], axis=0)
        return cw[0:1] * xm2 + cw[1:2] * xm1 + cw[2:3] * main + cw[3:4] * xp1 + cb

    def ctx_xr(t0):
        ref = xc_ref.at[0]
        main = ref[pl.ds(pl.multiple_of(t0 * nb, SUBLANES), tseg * nb), :]
        p0 = jnp.maximum(t0 * nb - 2 * nb, 0)
        prev = ref[pl.ds(pl.multiple_of(p0, SUBLANES), 2 * nb), :]
        n0 = jnp.minimum((t0 + tseg) * nb, lc * nb - nb)
        nxt = ref[pl.ds(pl.multiple_of(n0, SUBLANES), nb), :]
        return conv(jnp.where(t0 > 0, prev, 0.0), main, jnp.where(t0 + tseg < lc, nxt, 0.0))

    def lat_xr(sg):
        main = jnp.concatenate([xl_ref[0, sg, :, w * nb:(w + 1) * nb, :].reshape(rows * nb, hd)
                                for w in range(cpb)], axis=0)
        prev = xl_ref[0, jnp.maximum(sg - 1, 0), rows - 2:rows, (cpb - 1) * nb:cpb * nb, :].reshape(2 * nb, hd)
        nxt = xl_ref[0, jnp.minimum(sg + 1, nblk - 1), 0:1, 0:nb, :].reshape(nb, hd)
        return conv(jnp.where(sg > 0, prev, 0.0), main, jnp.where(sg < nblk - 1, nxt, 0.0))

    def gates(xr, direction, a_ref, u_ref):
        lo = direction * 2 * hd
        th_g = jnp.tanh(_dot(xr.astype(BF16), wg_ref[0, :, lo:lo + 2 * hd]) + gbias_ref[0, :, lo:lo + 2 * hd])
        hnsp = half_nsp[:, direction * hd:(direction + 1) * hd]
        log_a = th_g[:, :hd] * hnsp + hnsp
        ig = 0.5 * th_g[:, hd:] + 0.5
        a = jnp.exp(log_a)
        a_ref[...] = a
        sq = -jnp.tanh(log_a) * (1.0 + a * a)
        root = jnp.where(sq > 0.0, sq * lax.rsqrt(sq), 0.0)
        u_ref[...] = root * (ig * xr)

    def scan_segment(hf, hb, store, fseg, bseg):
        for blk in range(tseg // LRU_SCAN_BLOCK):
            for direction in range(N_DIR):
                a_ref, u_ref = (af, uf) if direction == 0 else (ab, ub)
                h0 = hf if direction == 0 else hb
                sg = fseg if direction == 0 else bseg
                pa = pu = None
                for kk in range(LRU_SCAN_BLOCK):
                    step = blk * LRU_SCAN_BLOCK + kk
                    t = step if direction == 0 else tseg - 1 - step
                    a = a_ref[t * nb:(t + 1) * nb, :]
                    u = u_ref[t * nb:(t + 1) * nb, :]
                    pa, pu = (a, u) if kk == 0 else (a * pa, a * pu + u)
                    h = pa * h0 + pu
                    w, r = t // rows, t % rows
                    if store == "set":
                        hsum[sg, r, w * nb:(w + 1) * nb, :] = h
                    elif store == "add":
                        hsum[sg, r, w * nb:(w + 1) * nb, :] = hsum[sg, r, w * nb:(w + 1) * nb, :] + h
                if direction == 0:
                    hf = h
                else:
                    hb = h
        return hf, hb

    h0 = jnp.zeros((nb, hd), F32)

    nseg_c = lc // tseg
    def ctx_body(i, carry):
        gates(ctx_xr(i * tseg), 0, af, uf)
        gates(ctx_xr((nseg_c - 1 - i) * tseg), 1, ab, ub)
        return scan_segment(carry[0], carry[1], None, 0, 0)
    hf, hb = lax.fori_loop(0, nseg_c, ctx_body, (h0, h0))

    def lat_body(store):
        def body(i, carry):
            bseg = nblk - 1 - i
            gates(lat_xr(i), 0, af, uf)
            gates(lat_xr(bseg), 1, ab, ub)
            return scan_segment(carry[0], carry[1], store, i, bseg)
        return body
    hf, hb = lax.fori_loop(0, nblk // 2, lat_body("set"), (hf, hb))
    lax.fori_loop(nblk // 2, nblk, lat_body("add"), (hf, hb))

    def out_body(rr, carry):
        for i in range(nb):
            for wg in range(nblk):
                o_ref[i, pl.ds(pl.multiple_of(rr * GRID_W + wg * cpb, SUBLANES), cpb), :] = (
                    hsum[wg, rr, pl.ds(i, cpb, stride=nb), :])
        return carry
    lax.fori_loop(0, rows, out_body, 0)


def _lru(xc, xl, conv_w, conv_b, wg, gbias, lam2, nb, lc, s):
    heads = xl.shape[0]
    rows = s // GRID_W
    tseg = SUBLANES * rows
    nblk = GRID_W // SUBLANES
    assert lc % tseg == 0 and nblk % 2 == 0 and tseg % LRU_SCAN_BLOCK == 0 and nb == SUBLANES
    return pl.pallas_call(
        functools.partial(_lru_kernel, lc=lc, s=s, rows=rows, nb=nb),
        grid=(heads,),
        in_specs=[pl.BlockSpec((1, lc * nb, LANES), lambda h: (h, 0, 0)),
                  pl.BlockSpec((1, nblk, rows, SUBLANES * nb, LANES), lambda h: (h, 0, 0, 0, 0)),
                  pl.BlockSpec((4, LANES), lambda h: (0, h)),
                  pl.BlockSpec((1, LANES), lambda h: (0, h)),
                  pl.BlockSpec((1, LANES, 4 * LANES), lambda h: (h, 0, 0)),
                  pl.BlockSpec((1, 1, 4 * LANES), lambda h: (h, 0, 0)),
                  pl.BlockSpec((1, 2 * LANES), lambda h: (0, h))],
        out_specs=pl.BlockSpec((nb, s, LANES), lambda h: (0, 0, h)),
        out_shape=jax.ShapeDtypeStruct((nb, s, heads * LANES), F32),
        scratch_shapes=[pltpu.VMEM((nblk, rows, SUBLANES * nb, LANES), F32)]
                       + [pltpu.VMEM((tseg * nb, LANES), F32)] * 4,
        compiler_params=_cparams(("parallel",)),
        name="lru",
    )(xc, xl, conv_w, conv_b, wg, gbias, lam2)


def _bmm(a, b):
    return lax.dot_general(a, b, (((2,), (1,)), ((0,), (0,))), preferred_element_type=F32)


def _bmm_nt(a, b):
    return lax.dot_general(a, b, (((2,), (2,)), ((0,), (0,))), preferred_element_type=F32)


def _inv_unit_triangular_x4(lm):
    n, c, w = lm.shape
    nblk = w // c
    ri = lax.broadcasted_iota(jnp.int32, (w, w), 0)
    ci = lax.broadcasted_iota(jnp.int32, (w, w), 1)
    on_diag_block = (ri // c) == (ci // c)
    eye = (lax.broadcasted_iota(jnp.int32, (c, w), 0)
           == lax.broadcasted_iota(jnp.int32, (c, w), 1) % c).astype(F32)

    def block_diag(x):
        return jnp.where(on_diag_block, jnp.concatenate([x] * nblk, axis=1), 0.0).astype(BF16)

    p = eye - lm
    lk = _bmm(lm.astype(BF16), block_diag(lm))
    n_sq = int(math.log2(c)) - 2
    for _ in range(n_sq):
        x = _bmm(jnp.concatenate([p, lk], axis=1).astype(BF16), block_diag(lk))
        p = p + x[:, :c]
        lk = x[:, c:]
    return p + _bmm(p.astype(BF16), block_diag(lk))


def _gdn_kernel(qc_ref, kc_ref, vc_ref, ql_ref, kl_ref, vl_ref, z_ref, g_ref, alog_ref, dtb_ref,
                cwq_ref, cwk_ref, cwv_ref, nw_ref, o_ref,
                qs, ks, vs, qcol, grow, grow2, xpad, aq_s, b_s, op_s, st_s, o_s, *, lc, s, hg, nbc):
    dk = LANES
    c = CHUNK
    l_all = lc + s
    ncc = lc // c
    ncl = s // c
    nct = ncc + ncl
    npb = nbc // 2

    ii = lax.broadcasted_iota(jnp.int32, (c, 2 * c), 0)
    jl = lax.broadcasted_iota(jnp.int32, (c, 2 * c), 1)
    fwd_half = jl < c
    jj = jnp.where(fwd_half, jl, jl - c)
    keep2 = (fwd_half & (ii >= jj)) | ((jl >= c) & (ii <= jj))
    strict2 = (fwd_half & (ii > jj)) | ((jl >= c) & (ii < jj))
    lane1 = lax.broadcasted_iota(jnp.int32, (1, LANES), 1)
    eye_k = lax.broadcasted_iota(jnp.int32, (dk, dk), 0) == lax.broadcasted_iota(jnp.int32, (dk, dk), 1)
    blk = 2 * LANES
    si = lax.broadcasted_iota(jnp.int32, (blk, blk), 0)
    ji = lax.broadcasted_iota(jnp.int32, (blk, blk), 1)
    same = (si // c) == (ji // c)
    t_pre = jnp.where(same & (si <= ji), 1.0, 0.0).astype(F32)
    t_suf = jnp.where(same & (si >= ji), 1.0, 0.0).astype(F32)

    def conv_silu(ref, hd, cw, n):
        p = SUBLANES
        zero = jnp.zeros((p, dk), F32)
        xpad[0:p] = zero
        xpad[p + n:2 * p + n] = zero
        xpad[p:p + n] = ref[0, hd].astype(F32)
        return _silu(cw[0:1] * xpad[p - 2:p - 2 + n] + cw[1:2] * xpad[p - 1:p - 1 + n]
                     + cw[2:3] * xpad[p:p + n] + cw[3:4] * xpad[p + 1:p + 1 + n])

    def l2n(t):
        return t * lax.rsqrt(jnp.sum(t * t, axis=-1, keepdims=True) + NORM_EPS)

    for hd in range(hg):
        cwq = cwq_ref[:, hd * dk:(hd + 1) * dk]
        cwk = cwk_ref[:, hd * dk:(hd + 1) * dk]
        cwv = cwv_ref[:, hd * dk:(hd + 1) * dk]
        qs[0:lc] = l2n(conv_silu(qc_ref, hd, cwq, lc)) * (dk ** -0.5)
        qs[lc:l_all] = l2n(conv_silu(ql_ref, hd, cwq, s)) * (dk ** -0.5)
        ks[0:lc] = l2n(conv_silu(kc_ref, hd, cwk, lc))
        ks[lc:l_all] = l2n(conv_silu(kl_ref, hd, cwk, s))
        vs[0:lc] = conv_silu(vc_ref, hd, cwv, lc)
        vs[lc:l_all] = conv_silu(vl_ref, hd, cwv, s)

        r = g_ref[0, hd]
        rid8 = lax.broadcasted_iota(jnp.int32, (SUBLANES, l_all), 0)
        aneg = -jnp.exp(alog_ref[hd][:, 0:1])
        dtb = dtb_ref[hd][:, 0:1]
        val = jnp.where(rid8 < 2, _sigmoid(r), aneg * _softplus(r + dtb))
        for i in range(l_all // blk):
            vb = val[:, i * blk:(i + 1) * blk]
            pre = jnp.dot(vb, t_pre, preferred_element_type=F32, precision=lax.Precision.HIGHEST)
            suf = jnp.dot(vb, t_suf, preferred_element_type=F32, precision=lax.Precision.HIGHEST)
            rb = lax.broadcasted_iota(jnp.int32, (SUBLANES, blk), 0)
            gblk = jnp.where(rb == 2, pre, jnp.where(rb == 3, suf, vb))
            grow[:, i * blk:(i + 1) * blk] = gblk
            for p2 in range(blk // LANES):
                g128 = gblk[:, p2 * LANES:(p2 + 1) * LANES]
                rolled = pltpu.roll(g128, c, 1)
                ch = (i * (blk // LANES) + p2) * 2
                grow2[0:1, ch * LANES:(ch + 1) * LANES] = jnp.where(lane1 < c, g128[2:3], rolled[3:4])
                grow2[0:1, (ch + 1) * LANES:(ch + 2) * LANES] = jnp.where(lane1 < c, rolled[2:3], g128[3:4])
        gfull = jnp.concatenate([grow[...], jnp.zeros((LANES - SUBLANES, l_all), F32)], axis=0)
        qcol[...] = gfull.T

        def intra(it, carry, hd=hd):
            r0 = pl.multiple_of(it * (nbc * c), 2 * c)
            q = qs[pl.ds(r0, nbc * c), :].reshape(nbc, c, dk)
            k = ks[pl.ds(r0, nbc * c), :].reshape(nbc, c, dk)
            v = vs[pl.ds(r0, nbc * c), :].reshape(nbc, c, dk)
            col = qcol[pl.ds(r0, nbc * c), :].reshape(nbc, c, LANES)
            g2 = grow2[0:1, pl.ds(pl.multiple_of(it * (nbc * LANES), LANES), nbc * LANES)]
            grw2 = jnp.stack([g2[:, j * LANES:(j + 1) * LANES] for j in range(nbc)], axis=0)
            beta = [col[:, :, d:d + 1] for d in range(N_DIR)]
            gcol = [col[:, :, 2 + d:3 + d] for d in range(N_DIR)]
            beta2 = jnp.where(fwd_half, beta[0], beta[1])
            gcol2 = jnp.where(fwd_half, gcol[0], gcol[1])
            kb16 = k.astype(BF16)
            both = _bmm_nt(jnp.concatenate([q.astype(BF16), kb16], axis=1),
                           jnp.concatenate([kb16, kb16], axis=1))
            decay2 = jnp.exp(jnp.where(keep2, gcol2 - grw2, NEG_BIG))
            lm2 = jnp.where(strict2, beta2 * both[:, c:] * decay2, 0.0)
            qkd2 = (both[:, :c] * decay2).astype(BF16)
            half = nbc // 2
            tinv4 = _inv_unit_triangular_x4(jnp.concatenate([lm2[:half], lm2[half:]], axis=2))
            tinv2 = jnp.concatenate([tinv4[:, :, :2 * c], tinv4[:, :, 2 * c:]], axis=0)
            eg = [jnp.exp(g) for g in gcol]
            glast = [gcol[0][:, c - 1:c, :], gcol[1][:, 0:1, :]]
            rhs = [jnp.concatenate([v * beta[d], (k * beta[d]) * eg[d]], axis=2).astype(BF16) for d in range(N_DIR)]
            zero = jnp.zeros((nbc, c, 2 * dk), BF16)
            rhs_bd = jnp.concatenate([jnp.concatenate([rhs[0], zero], axis=2),
                                      jnp.concatenate([zero, rhs[1]], axis=2)], axis=1)
            uw2 = _bmm(tinv2.astype(BF16), rhs_bd).astype(BF16)
            uw_bd = jnp.concatenate([jnp.concatenate([uw2[:, :, :2 * dk], zero], axis=2),
                                     jnp.concatenate([zero, uw2[:, :, 2 * dk:]], axis=2)], axis=1)
            kd = jnp.concatenate([k * jnp.exp(glast[d] - gcol[d]) for d in range(N_DIR)], axis=1)
            kdt2 = jnp.swapaxes(kd, 1, 2).astype(BF16)
            m2 = _bmm(jnp.concatenate([kdt2, qkd2], axis=1), uw_bd)
            def split_pairs(t):
                t = t.reshape((npb, 2) + t.shape[1:])
                return t[:, 0], t[:, 1]
            for d in range(N_DIR):
                m = m2[:, :, d * 2 * dk:(d + 1) * 2 * dk]
                a_mat = jnp.where(eye_k, jnp.exp(glast[d]), 0.0) - m[:, :dk, dk:]
                q_mat = q * eg[d] - m[:, dk:, dk:]
                parts = [split_pairs(t) for t in (a_mat, m[:, :dk, :dk], q_mat, m[:, dk:, :dk])]
                (a1, b1, q1, o1), (a2, b2, q2, o2) = [[p[first ^ d] for p in parts] for first in (0, 1)]
                mm = _bmm(jnp.concatenate([a2, q2], axis=1).astype(BF16),
                          jnp.concatenate([a1, b1], axis=2).astype(BF16))
                q21 = mm[:, dk:, :dk]
                o21 = mm[:, dk:, dk:] + o2
                q_tok = (q1, q21) if d == 0 else (q21, q1)
                o_tok = (o1, o21) if d == 0 else (o21, o1)
                aq_s[hd, d, pl.ds(it * npb, npb)] = jnp.concatenate((mm[:, :dk, :dk],) + q_tok, axis=1).astype(BF16)
                b_s[hd, d, pl.ds(it * npb, npb)] = (mm[:, :dk, dk:] + b2).astype(BF16)
                op_s[hd, d, pl.ds(it * npb, npb)] = jnp.concatenate(o_tok, axis=1).astype(BF16)
            return carry
        for it in range(nct // nbc):
            intra(it, 0)

    st_s[...] = jnp.zeros_like(st_s)
    o_s[...] = jnp.zeros_like(o_s)
    ncp = ncc // 2
    nlp = ncl // 2

    def recur(t, carry):
        pf = t
        pb = jnp.where(t < ncp, ncp - 1 - t, 2 * ncp + nlp - 1 - t)
        chains = [(hd, d, pi) for hd in range(hg) for d, pi in ((0, pf), (1, pb))]
        loaded = []
        for hd, d, pi in chains:
            ro = pl.multiple_of(jnp.where(t >= ncp, pi * (2 * c) - lc, s), 2 * c)
            loaded.append((aq_s[hd, d, pi], st_s[hd, d], b_s[hd, d, pi], op_s[hd, d, pi],
                           o_s[hd, pl.ds(ro, 2 * c), :], ro))
        results = []
        for aq, st, bm, om, o_old, ro in loaded:
            x = _dot(aq, st.astype(BF16))
            results.append((x[:dk] + bm.astype(F32), o_old + x[dk:] + om.astype(F32), ro))
        for (hd, d, pi), (st_new, o_new, ro) in zip(chains, results):
            st_s[hd, d] = st_new
            o_s[hd, pl.ds(ro, 2 * c), :] = o_new
        return carry
    lax.fori_loop(0, ncp + nlp, recur, 0)

    for hd in range(hg):
        z = z_ref[0, hd].astype(F32)
        o_ref[0, :, hd * dk:(hd + 1) * dk] = (
            (_rms_rows(o_s[hd, 0:s]) * nw_ref[...]).astype(F32) * _silu(z)).astype(o_ref.dtype)


def _gdn(pc, plat, grow, alog8, dtb8, conv_w, norm_w, lc, s):
    b = plat.shape[0]
    heads = GDN_HEADS
    l_all = lc + s
    nct = l_all // CHUNK
    assert lc % CHUNK == 0 and s % CHUNK == 0 and l_all % (2 * LANES) == 0
    assert GDN_CHUNK_BATCH % 2 == 0 and nct % GDN_CHUNK_BATCH == 0
    assert lc % (2 * CHUNK) == 0 and s % (4 * CHUNK) == 0
    hg = GDN_HEADS_PER_STEP
    assert heads % hg == 0
    def slot(base):
        return lambda i, h: (i, base // hg + h, 0, 0)
    def cw(base):
        return lambda i, h: (0, base // hg + h)
    return pl.pallas_call(
        functools.partial(_gdn_kernel, lc=lc, s=s, hg=hg, nbc=GDN_CHUNK_BATCH),
        grid=(b, heads // hg),
        in_specs=[pl.BlockSpec((1, hg, lc, LANES), slot(0)),
                  pl.BlockSpec((1, hg, lc, LANES), slot(heads)),
                  pl.BlockSpec((1, hg, lc, LANES), slot(2 * heads)),
                  pl.BlockSpec((1, hg, s, LANES), slot(heads)),
                  pl.BlockSpec((1, hg, s, LANES), slot(2 * heads)),
                  pl.BlockSpec((1, hg, s, LANES), slot(3 * heads)),
                  pl.BlockSpec((1, hg, s, LANES), slot(4 * heads)),
                  pl.BlockSpec((1, hg, SUBLANES, l_all), lambda i, h: (i, h, 0, 0)),
                  pl.BlockSpec((hg, SUBLANES, LANES), lambda i, h: (h, 0, 0)),
                  pl.BlockSpec((hg, SUBLANES, LANES), lambda i, h: (h, 0, 0)),
                  pl.BlockSpec((4, hg * LANES), cw(0)),
                  pl.BlockSpec((4, hg * LANES), cw(heads)),
                  pl.BlockSpec((4, hg * LANES), cw(2 * heads)),
                  pl.BlockSpec((1, LANES), lambda i, h: (0, 0))],
        out_specs=pl.BlockSpec((1, s, hg * LANES), lambda i, h: (i, 0, h)),
        out_shape=jax.ShapeDtypeStruct((b, s, heads * LANES), BF16),
        scratch_shapes=[pltpu.VMEM((l_all, LANES), F32),
                        pltpu.VMEM((l_all, LANES), F32),
                        pltpu.VMEM((l_all, LANES), F32),
                        pltpu.VMEM((l_all, LANES), F32),
                        pltpu.VMEM((SUBLANES, l_all), F32),
                        pltpu.VMEM((SUBLANES, nct * LANES), F32),
                        pltpu.VMEM((s + 2 * SUBLANES, LANES), F32),
                        pltpu.VMEM((hg, N_DIR, nct // 2, LANES + 2 * CHUNK, LANES), BF16),
                        pltpu.VMEM((hg, N_DIR, nct // 2, LANES, LANES), BF16),
                        pltpu.VMEM((hg, N_DIR, nct // 2, 2 * CHUNK, LANES), BF16),
                        pltpu.VMEM((hg, N_DIR, LANES, LANES), F32),
                        pltpu.VMEM((hg, s + 2 * CHUNK, LANES), F32)],
        compiler_params=_cparams(("parallel", "arbitrary")),
        name="gdn",
    )(pc, pc, pc, plat, plat, plat, plat, grow, alog8, dtb8, conv_w, conv_w, conv_w, norm_w)


def _out_kernel(hs_ref, y_ref, gdn_ref, w_ref, x_ref, gm_ref, scf_ref, shf_ref, nw1_ref, nw2_ref,
                x1_ref, h2_ref, *, heads):
    half = heads * LANES
    a2 = nw2_ref[...] * (1.0 + scf_ref[0])
    tm = x_ref.shape[1]
    for r0 in range(0, tm, OUT_SUB_ROWS):
        rs = slice(r0, r0 + OUT_SUB_ROWS)
        y = jnp.concatenate([y_ref[0, j, rs, :] for j in range(heads)], axis=-1).astype(F32)
        lru = (hs_ref[0, rs, :] * _gelu_tanh(y)).astype(BF16)
        m = _dot(lru, w_ref[0:half, :]) + _dot(gdn_ref[0, rs, :], w_ref[half:, :])
        x1 = x_ref[0, rs, :] + gm_ref[0] * (_rms_rows(m) * nw1_ref[...])
        x1_ref[0, rs, :] = x1
        h2_ref[0, rs, :] = (_rms_rows(x1) * a2 + shf_ref[0]).astype(BF16)


def _out_proj(hs, plat, gdn, w_out, x, gm, scf, shf, nw1, nw2, tm):
    b, s, d = x.shape
    heads = LRU_HEADS
    dm = w_out.shape[0]
    row = lambda i, m: (i, m, 0)
    vec = lambda i, m: (i, 0, 0)
    fix = lambda i, m: (0, 0)
    return pl.pallas_call(
        functools.partial(_out_kernel, heads=heads),
        grid=(b, s // tm),
        in_specs=[pl.BlockSpec((1, tm, heads * LANES), row),
                  pl.BlockSpec((1, heads, tm, LANES), lambda i, m: (i, 0, m, 0)),
                  pl.BlockSpec((1, tm, dm - heads * LANES), row),
                  pl.BlockSpec((dm, d), fix),
                  pl.BlockSpec((1, tm, d), row),
                  pl.BlockSpec((1, 1, d), vec),
                  pl.BlockSpec((1, 1, d), vec),
                  pl.BlockSpec((1, 1, d), vec),
                  pl.BlockSpec((1, d), fix),
                  pl.BlockSpec((1, d), fix)],
        out_specs=[pl.BlockSpec((1, tm, d), row), pl.BlockSpec((1, tm, d), row)],
        out_shape=[jax.ShapeDtypeStruct((b, s, d), F32), jax.ShapeDtypeStruct((b, s, d), BF16)],
        compiler_params=_cparams(("parallel", "parallel")),
        name="out_proj",
    )(hs, plat, gdn, w_out, x, gm, scf, shf, nw1, nw2)


def _mlp_kernel(h_ref, w1_ref, w2_ref, x1_ref, gf_ref, nw_ref, o_ref):
    f = pl.program_id(2)

    @pl.when(f == 0)
    def _():
        o_ref[0] = jnp.zeros_like(o_ref[0])

    hid = jnp.maximum(_dot(h_ref[0], w1_ref[...]), 0.0)
    o_ref[0] = o_ref[0] + _dot((hid * hid).astype(BF16), w2_ref[...])

    @pl.when(f == pl.num_programs(2) - 1)
    def _():
        o_ref[0] = x1_ref[0] + gf_ref[0] * (_rms_rows(o_ref[0]) * nw_ref[...])


def _mlp(h2, w1, w2, x1, gf, nw3, tm, tf):
    b, s, d = x1.shape
    ff = w1.shape[1]
    row = lambda i, m, f: (i, m, 0)
    return pl.pallas_call(
        _mlp_kernel,
        grid=(b, s // tm, ff // tf),
        in_specs=[pl.BlockSpec((1, tm, d), row),
                  pl.BlockSpec((d, tf), lambda i, m, f: (0, f)),
                  pl.BlockSpec((tf, d), lambda i, m, f: (f, 0)),
                  pl.BlockSpec((1, tm, d), row),
                  pl.BlockSpec((1, 1, d), lambda i, m, f: (i, 0, 0)),
                  pl.BlockSpec((1, d), lambda i, m, f: (0, 0))],
        out_specs=pl.BlockSpec((1, tm, d), row),
        out_shape=jax.ShapeDtypeStruct((b, s, d), F32),
        compiler_params=_cparams(("parallel", "parallel", "arbitrary"), VMEM_LIMIT_MLP),
        name="mlp",
    )(h2, w1, w2, x1, gf, nw3)


def kernel(x, c, ctx, c_ctx, w_mod, b_mod, norm_w, w_in, lru_conv_w, lru_conv_b, lru_gate_w, lru_gate_b,
           lru_lambda, gdn_conv_w, gdn_a_log, gdn_dt_bias, gdn_norm_w, w_out, w_ff1, w_ff2):
    b, s, d = x.shape
    lc = ctx.shape[1]
    l_all = lc + s
    rows = s // GRID_W
    d_lru = LRU_HEADS * LANES
    d_gdn = GDN_HEADS * LANES
    assert w_mod.shape[0] == 1 and b == SUBLANES and d == d_lru + d_gdn

    cc = jnp.concatenate([c, c_ctx[None], jnp.zeros((2 * SUBLANES - b - 1, d), F32)], axis=0)
    mod = _modulation(cc, w_mod[0], b_mod[0])
    sh_m, sc_m, g_m, sh_f, sc_f, g_f = [mod[:b, i * d:(i + 1) * d] for i in range(6)]
    csh_m = jnp.broadcast_to(mod[b:b + 1, 0:d], (b, d))
    csc_m = jnp.broadcast_to(mod[b:b + 1, d:2 * d], (b, d))
    nw = norm_w[0]
    nw0, nw1, nw2, nw3 = [nw[i:i + 1] for i in range(4)]
    v3 = lambda t: t.reshape(b, 1, d)

    w_in0 = w_in[0]
    y_end = 2 * d_lru
    qkv_end = y_end + 3 * d_gdn
    z_end = qkv_end + d_gdn
    n_gb = w_in0.shape[1] - z_end
    w_gb = jnp.pad(w_in0[:, z_end:], ((0, 0), (0, LANES - n_gb))).astype(BF16)

    p_lat, gb_lat = _inproj_raster(x, nw0, v3(sc_m), v3(sh_m), w_in0, w_gb, col0=d_lru, n=z_end - d_lru,
                                   tm=1024, tn=1024)
    p_ctx, gb_ctx = _inproj_raster(ctx, nw0, v3(csc_m), v3(csh_m), w_in0, w_gb, col0=y_end, n=qkv_end - y_end,
                                   tm=lc, tn=1024)
    xl = _inproj_tb(x.reshape(b, rows, GRID_W, d), nw0, v3(sc_m), v3(sh_m), w_in0, n=d_lru, r=rows, nbh=b // 2)
    xl = xl.reshape(LRU_HEADS, GRID_W // SUBLANES, rows, SUBLANES * b, LANES)
    xc = _inproj_tb(ctx.reshape(b, lc // SUBLANES, SUBLANES, d), nw0, v3(csc_m), v3(csh_m), w_in0, n=d_lru,
                    r=2 * SUBLANES, nbh=b)

    gw = lru_gate_w[0]
    wg = (0.5 * jnp.transpose(gw, (2, 3, 0, 1, 4))).reshape(LRU_HEADS, LANES, 4 * LANES).astype(BF16)
    gbias = 0.5 * jnp.transpose(lru_gate_b[0].reshape(N_DIR, 2, LRU_HEADS, LANES), (2, 0, 1, 3))
    gbias = gbias.reshape(LRU_HEADS, 1, 4 * LANES)
    lam2 = jnp.transpose(lru_lambda[0].reshape(N_DIR, LRU_HEADS, LANES), (1, 0, 2)).reshape(1, 2 * d_lru)
    hs = _lru(xc, xl, lru_conv_w[0], lru_conv_b[0].reshape(1, d_lru), wg, gbias, lam2, b, lc, s)

    gb = jnp.concatenate([gb_ctx, gb_lat], axis=1)[:, :, :n_gb]
    gb = gb.reshape(b, l_all, 2, N_DIR, GDN_HEADS)
    grow = jnp.transpose(gb, (0, 4, 2, 3, 1)).reshape(b, GDN_HEADS, 2 * N_DIR, l_all)
    grow = jnp.pad(grow, ((0, 0), (0, 0), (0, SUBLANES - 2 * N_DIR), (0, 0)))
    def rows8(t):
        t = jnp.transpose(t, (1, 0))[:, :, None]
        t = jnp.pad(t, ((0, 0), (2, SUBLANES - 2 - N_DIR), (0, 0)))
        return jnp.broadcast_to(t, (GDN_HEADS, SUBLANES, LANES)).astype(F32)
    gdn = _gdn(p_ctx, p_lat, grow, rows8(gdn_a_log[0]), rows8(gdn_dt_bias[0]), gdn_conv_w[0],
               gdn_norm_w[0].reshape(1, LANES), lc, s)

    x1, h2 = _out_proj(hs, p_lat, gdn, w_out[0].astype(BF16), x, v3(g_m), v3(sc_f), v3(sh_f),
                       nw1, nw2, tm=512)
    return _mlp(h2, w_ff1[0].astype(BF16), w_ff2[0].astype(BF16), x1, v3(g_f), nw3, tm=512, tf=2048)
```

```python
import functools
import math

import jax
import jax.numpy as jnp
from jax import lax
from jax.experimental import pallas as pl
from jax.experimental.pallas import tpu as pltpu

F32 = jnp.float32
BF16 = jnp.bfloat16

LANES = 128
SUBLANES = 8
VMEM_LIMIT = 56 * 1024 * 1024
VMEM_LIMIT_MLP = 62 * 1024 * 1024

GRID_W = 64
NORM_EPS = 1e-6
LRU_C = 8.0
LRU_HEADS = 8
GDN_HEADS = 8
N_DIR = 2
CHUNK = 64
NEG_BIG = -1e30
OUT_SUB_ROWS = 256
LRU_SCAN_BLOCK = 8
GDN_HEADS_PER_STEP = 2
GDN_CHUNK_BATCH = 18


def _cparams(sem, vmem_limit=VMEM_LIMIT):
    return pltpu.CompilerParams(dimension_semantics=sem, vmem_limit_bytes=vmem_limit)


def _dot(a, b):
    return jnp.dot(a, b, preferred_element_type=F32)


def _dot_nt(a, b):
    return lax.dot_general(a, b, (((1,), (1,)), ((), ())), preferred_element_type=F32)


def _sigmoid(t):
    return 0.5 * jnp.tanh(0.5 * t) + 0.5


def _softplus(t):
    return jnp.maximum(t, 0.0) + jnp.log(1.0 + jnp.exp(-jnp.abs(t)))


def _silu(t):
    return t * _sigmoid(t)


def _gelu_tanh(t):
    return 0.5 * t * (1.0 + jnp.tanh(math.sqrt(2.0 / math.pi) * (t + 0.044715 * (t * t * t))))


def _rms_rows(t):
    return t * lax.rsqrt(jnp.mean(t * t, axis=-1, keepdims=True) + NORM_EPS)


def _mod_kernel(c_ref, w_ref, b_ref, o_ref):
    s = _silu(c_ref[...])
    o_ref[...] = _dot(s.astype(BF16), w_ref[...].astype(BF16)) + b_ref[...]


def _modulation(cc, w_mod, b_mod):
    rows, d = cc.shape
    n = w_mod.shape[1]
    tn = 1024
    return pl.pallas_call(
        _mod_kernel,
        grid=(n // tn,),
        in_specs=[pl.BlockSpec((rows, d), lambda j: (0, 0)),
                  pl.BlockSpec((d, tn), lambda j: (0, j)),
                  pl.BlockSpec((1, tn), lambda j: (0, j))],
        out_specs=pl.BlockSpec((rows, tn), lambda j: (0, j)),
        out_shape=jax.ShapeDtypeStruct((rows, n), F32),
        compiler_params=_cparams(("arbitrary",)),
        name="mod",
    )(cc, w_mod, b_mod.reshape(1, n))


def _inproj_raster_kernel(x_ref, nw_ref, sc_ref, sh_ref, w_ref, wgb_ref, p_ref, gb_ref, h_scr, *, tn):
    @pl.when(pl.program_id(2) == 0)
    def _():
        a = nw_ref[...] * (1.0 + sc_ref[0])
        h = (_rms_rows(x_ref[0]) * a + sh_ref[0]).astype(BF16)
        h_scr[...] = h
        gb_ref[0] = _dot(h, wgb_ref[...])

    acc = _dot(h_scr[...], w_ref[...])
    for j in range(tn // LANES):
        p_ref[0, j] = acc[:, j * LANES:(j + 1) * LANES].astype(p_ref.dtype)


def _inproj_raster(x, nw, sc, sh, w, wgb, col0, n, tm, tn):
    b, l, d = x.shape
    assert col0 % tn == 0 and n % tn == 0
    off = col0 // tn
    return pl.pallas_call(
        functools.partial(_inproj_raster_kernel, tn=tn),
        grid=(b, l // tm, n // tn),
        in_specs=[pl.BlockSpec((1, tm, d), lambda i, m, j: (i, m, 0)),
                  pl.BlockSpec((1, d), lambda i, m, j: (0, 0)),
                  pl.BlockSpec((1, 1, d), lambda i, m, j: (i, 0, 0)),
                  pl.BlockSpec((1, 1, d), lambda i, m, j: (i, 0, 0)),
                  pl.BlockSpec((d, tn), lambda i, m, j: (0, j + off)),
                  pl.BlockSpec((d, LANES), lambda i, m, j: (0, 0))],
        out_specs=[pl.BlockSpec((1, tn // LANES, tm, LANES), lambda i, m, j: (i, j, m, 0)),
                   pl.BlockSpec((1, tm, LANES), lambda i, m, j: (i, m, 0))],
        out_shape=[jax.ShapeDtypeStruct((b, n // LANES, l, LANES), BF16),
                   jax.ShapeDtypeStruct((b, l, LANES), F32)],
        scratch_shapes=[pltpu.VMEM((tm, d), BF16)],
        compiler_params=_cparams(("parallel", "parallel", "arbitrary")),
        name="inproj_raster",
    )(x, nw, sc, sh, w, wgb)


def _inproj_tb_kernel(x_ref, nw_ref, sc_ref, sh_ref, w_ref, o_ref, *, nbh, nb):
    bh = pl.program_id(1)
    per_b = x_ref.shape[1] * x_ref.shape[2]
    d = x_ref.shape[3]
    for i in range(nbh):
        a = nw_ref[...] * (1.0 + sc_ref[i])
        h = _rms_rows(x_ref[i].reshape(per_b, d)) * a + sh_ref[i]
        acc = _dot(h.astype(BF16), w_ref[...])
        for j in range(w_ref.shape[1] // LANES):
            o_ref[j, pl.ds(bh * nbh + i, per_b, stride=nb), :] = acc[:, j * LANES:(j + 1) * LANES]


def _inproj_tb(xv, nw, sc, sh, w, n, r, nbh):
    nb, r_total, c_total, d = xv.shape
    cw = SUBLANES
    assert nb % nbh == 0 and r_total % r == 0 and c_total % cw == 0 and (c_total == cw or r_total == r)
    per_b = r * cw
    n_m = (r_total // r) * (c_total // cw)
    x_map = (lambda m, bh: (bh, m, 0, 0)) if c_total == cw else (lambda m, bh: (bh, 0, m, 0))
    return pl.pallas_call(
        functools.partial(_inproj_tb_kernel, nbh=nbh, nb=nb),
        grid=(n_m, nb // nbh),
        in_specs=[pl.BlockSpec((nbh, r, cw, d), x_map),
                  pl.BlockSpec((1, d), lambda m, bh: (0, 0)),
                  pl.BlockSpec((nbh, 1, d), lambda m, bh: (bh, 0, 0)),
                  pl.BlockSpec((nbh, 1, d), lambda m, bh: (bh, 0, 0)),
                  pl.BlockSpec((d, n), lambda m, bh: (0, 0))],
        out_specs=pl.BlockSpec((n // LANES, per_b * nb, LANES), lambda m, bh: (0, m, 0)),
        out_shape=jax.ShapeDtypeStruct((n // LANES, n_m * per_b * nb, LANES), F32),
        compiler_params=_cparams(("parallel", "arbitrary")),
        name="inproj_tb",
    )(xv, nw, sc, sh, w)


def _lru_kernel(xc_ref, xl_ref, cw_ref, cb_ref, wg_ref, gbias_ref, lam_ref, o_ref,
                hsum, af, uf, ab, ub, xr_c, *, lc, s, rows, nb):
    hd = LANES
    cpb = SUBLANES
    tseg = cpb * rows
    nblk = GRID_W // cpb
    cw = cw_ref[...]
    cb = cb_ref[...]
    half_nsp = (-0.5 * LRU_C) * _softplus(-lam_ref[...])

    def conv(prev, main, nxt):
        xm2 = jnp.concatenate([prev, main[:-2 * nb]], axis=0)
        xm1 = jnp.concatenate([prev[nb:], main[:-nb]], axis=0)
        xp1 = jnp.concatenate([main[nb:], nxt], axis=0)
        return cw[0:1] * xm2 + cw[1:2] * xm1 + cw[2:3] * main + cw[3:4] * xp1 + cb

    def ctx_xr(t0):
        ref = xc_ref.at[0]
        main = ref[pl.ds(pl.multiple_of(t0 * nb, SUBLANES), tseg * nb), :]
        p0 = jnp.maximum(t0 * nb - 2 * nb, 0)
        prev = ref[pl.ds(pl.multiple_of(p0, SUBLANES), 2 * nb), :]
        n0 = jnp.minimum((t0 + tseg) * nb, lc * nb - nb)
        nxt = ref[pl.ds(pl.multiple_of(n0, SUBLANES), nb), :]
        return conv(jnp.where(t0 > 0, prev, 0.0), main, jnp.where(t0 + tseg < lc, nxt, 0.0))

    def lat_xr(sg):
        main = jnp.concatenate([xl_ref[0, sg, :, w * nb:(w + 1) * nb, :].reshape(rows * nb, hd)
                                for w in range(cpb)], axis=0)
        prev = xl_ref[0, jnp.maximum(sg - 1, 0), rows - 2:rows, (cpb - 1) * nb:cpb * nb, :].reshape(2 * nb, hd)
        nxt = xl_ref[0, jnp.minimum(sg + 1, nblk - 1), 0:1, 0:nb, :].reshape(nb, hd)
        return conv(jnp.where(sg > 0, prev, 0.0), main, jnp.where(sg < nblk - 1, nxt, 0.0))

    def gates(xr, direction, a_ref, u_ref):
        lo = direction * 2 * hd
        th_g = jnp.tanh(_dot(xr.astype(BF16), wg_ref[0, :, lo:lo + 2 * hd]) + gbias_ref[0, :, lo:lo + 2 * hd])
        hnsp = half_nsp[:, direction * hd:(direction + 1) * hd]
        log_a = th_g[:, :hd] * hnsp + hnsp
        ig = 0.5 * th_g[:, hd:] + 0.5
        a = jnp.exp(log_a)
        a_ref[...] = a
        sq = -jnp.tanh(log_a) * (1.0 + a * a)
        root = jnp.where(sq > 0.0, sq * lax.rsqrt(sq), 0.0)
        u_ref[...] = root * (ig * xr)

    def scan_segment(hf, hb, store, fseg, bseg):
        for blk in range(tseg // LRU_SCAN_BLOCK):
            for direction in range(N_DIR):
                a_ref, u_ref = (af, uf) if direction == 0 else (ab, ub)
                h0 = hf if direction == 0 else hb
                sg = fseg if direction == 0 else bseg
                pa = pu = None
                for kk in range(LRU_SCAN_BLOCK):
                    step = blk * LRU_SCAN_BLOCK + kk
                    t = step if direction == 0 else tseg - 1 - step
                    a = a_ref[t * nb:(t + 1) * nb, :]
                    u = u_ref[t * nb:(t + 1) * nb, :]
                    pa, pu = (a, u) if kk == 0 else (a * pa, a * pu + u)
                    h = pa * h0 + pu
                    w, r = t // rows, t % rows
                    if store == "set":
                        hsum[sg, r, w * nb:(w + 1) * nb, :] = h
                    elif store == "add":
                        hsum[sg, r, w * nb:(w + 1) * nb, :] = hsum[sg, r, w * nb:(w + 1) * nb, :] + h
                if direction == 0:
                    hf = h
                else:
                    hb = h
        return hf, hb

    h0 = jnp.zeros((nb, hd), F32)

    nseg_c = lc // tseg
    def ctx_body(i, carry):
        gates(ctx_xr(i * tseg), 0, af, uf)
        gates(ctx_xr((nseg_c - 1 - i) * tseg), 1, ab, ub)
        return scan_segment(carry[0], carry[1], None, 0, 0)
    hf, hb = lax.fori_loop(0, nseg_c, ctx_body, (h0, h0))

    def lat_body(store):
        def body(i, carry):
            bseg = nblk - 1 - i
            if store == "set":
                xr_f, xr_b = lat_xr(i), lat_xr(bseg)
                xr_c[i] = xr_f
                xr_c[bseg] = xr_b
            else:
                xr_f, xr_b = xr_c[i], xr_c[bseg]
            gates(xr_f, 0, af, uf)
            gates(xr_b, 1, ab, ub)
            return scan_segment(carry[0], carry[1], store, i, bseg)
        return body
    hf, hb = lax.fori_loop(0, nblk // 2, lat_body("set"), (hf, hb))
    lax.fori_loop(nblk // 2, nblk, lat_body("add"), (hf, hb))

    def out_body(rr, carry):
        for i in range(nb):
            for wg in range(nblk):
                o_ref[i, pl.ds(pl.multiple_of(rr * GRID_W + wg * cpb, SUBLANES), cpb), :] = (
                    hsum[wg, rr, pl.ds(i, cpb, stride=nb), :])
        return carry
    lax.fori_loop(0, rows, out_body, 0)


def _lru(xc, xl, conv_w, conv_b, wg, gbias, lam2, nb, lc, s):
    heads = xl.shape[0]
    rows = s // GRID_W
    tseg = SUBLANES * rows
    nblk = GRID_W // SUBLANES
    assert lc % tseg == 0 and nblk % 2 == 0 and tseg % LRU_SCAN_BLOCK == 0 and nb == SUBLANES
    return pl.pallas_call(
        functools.partial(_lru_kernel, lc=lc, s=s, rows=rows, nb=nb),
        grid=(heads,),
        in_specs=[pl.BlockSpec((1, lc * nb, LANES), lambda h: (h, 0, 0)),
                  pl.BlockSpec((1, nblk, rows, SUBLANES * nb, LANES), lambda h: (h, 0, 0, 0, 0)),
                  pl.BlockSpec((4, LANES), lambda h: (0, h)),
                  pl.BlockSpec((1, LANES), lambda h: (0, h)),
                  pl.BlockSpec((1, LANES, 4 * LANES), lambda h: (h, 0, 0)),
                  pl.BlockSpec((1, 1, 4 * LANES), lambda h: (h, 0, 0)),
                  pl.BlockSpec((1, 2 * LANES), lambda h: (0, h))],
        out_specs=pl.BlockSpec((nb, s, LANES), lambda h: (0, 0, h)),
        out_shape=jax.ShapeDtypeStruct((nb, s, heads * LANES), F32),
        scratch_shapes=[pltpu.VMEM((nblk, rows, SUBLANES * nb, LANES), F32)]
                       + [pltpu.VMEM((tseg * nb, LANES), F32)] * 4
                       + [pltpu.VMEM((nblk, tseg * nb, LANES), F32)],
        compiler_params=_cparams(("parallel",), VMEM_LIMIT_MLP),
        name="lru",
    )(xc, xl, conv_w, conv_b, wg, gbias, lam2)


def _bmm(a, b):
    return lax.dot_general(a, b, (((2,), (1,)), ((0,), (0,))), preferred_element_type=F32)


def _bmm_nt(a, b):
    return lax.dot_general(a, b, (((2,), (2,)), ((0,), (0,))), preferred_element_type=F32)


def _inv_unit_triangular_x4(lm):
    n, c, w = lm.shape
    nblk = w // c
    ri = lax.broadcasted_iota(jnp.int32, (w, w), 0)
    ci = lax.broadcasted_iota(jnp.int32, (w, w), 1)
    on_diag_block = (ri // c) == (ci // c)
    eye = (lax.broadcasted_iota(jnp.int32, (c, w), 0)
           == lax.broadcasted_iota(jnp.int32, (c, w), 1) % c).astype(F32)

    def block_diag(x):
        return jnp.where(on_diag_block, jnp.concatenate([x] * nblk, axis=1), 0.0).astype(BF16)

    p = eye - lm
    lk = _bmm(lm.astype(BF16), block_diag(lm))
    n_sq = int(math.log2(c)) - 2
    for _ in range(n_sq):
        x = _bmm(jnp.concatenate([p, lk], axis=1).astype(BF16), block_diag(lk))
        p = p + x[:, :c]
        lk = x[:, c:]
    return p + _bmm(p.astype(BF16), block_diag(lk))


def _gdn_kernel(qc_ref, kc_ref, vc_ref, ql_ref, kl_ref, vl_ref, z_ref, g_ref, alog_ref, dtb_ref,
                cwq_ref, cwk_ref, cwv_ref, nw_ref, o_ref,
                qs, ks, vs, qcol, grow, grow2, xpad, aq_s, b_s, op_s, st_s, o_s, *, lc, s, hg, nbc):
    dk = LANES
    c = CHUNK
    l_all = lc + s
    ncc = lc // c
    ncl = s // c
    nct = ncc + ncl
    npb = nbc // 2

    ii = lax.broadcasted_iota(jnp.int32, (c, 2 * c), 0)
    jl = lax.broadcasted_iota(jnp.int32, (c, 2 * c), 1)
    fwd_half = jl < c
    jj = jnp.where(fwd_half, jl, jl - c)
    keep2 = (fwd_half & (ii >= jj)) | ((jl >= c) & (ii <= jj))
    strict2 = (fwd_half & (ii > jj)) | ((jl >= c) & (ii < jj))
    lane1 = lax.broadcasted_iota(jnp.int32, (1, LANES), 1)
    eye_k = lax.broadcasted_iota(jnp.int32, (dk, dk), 0) == lax.broadcasted_iota(jnp.int32, (dk, dk), 1)
    blk = 2 * LANES
    si = lax.broadcasted_iota(jnp.int32, (blk, blk), 0)
    ji = lax.broadcasted_iota(jnp.int32, (blk, blk), 1)
    same = (si // c) == (ji // c)
    t_pre = jnp.where(same & (si <= ji), 1.0, 0.0).astype(F32)
    t_suf = jnp.where(same & (si >= ji), 1.0, 0.0).astype(F32)

    def conv_silu(ref, hd, cw, n):
        p = SUBLANES
        zero = jnp.zeros((p, dk), F32)
        xpad[0:p] = zero
        xpad[p + n:2 * p + n] = zero
        xpad[p:p + n] = ref[0, hd].astype(F32)
        return _silu(cw[0:1] * xpad[p - 2:p - 2 + n] + cw[1:2] * xpad[p - 1:p - 1 + n]
                     + cw[2:3] * xpad[p:p + n] + cw[3:4] * xpad[p + 1:p + 1 + n])

    def l2n(t):
        return t * lax.rsqrt(jnp.sum(t * t, axis=-1, keepdims=True) + NORM_EPS)

    for hd in range(hg):
        cwq = cwq_ref[:, hd * dk:(hd + 1) * dk]
        cwk = cwk_ref[:, hd * dk:(hd + 1) * dk]
        cwv = cwv_ref[:, hd * dk:(hd + 1) * dk]
        qs[0:lc] = l2n(conv_silu(qc_ref, hd, cwq, lc)) * (dk ** -0.5)
        qs[lc:l_all] = l2n(conv_silu(ql_ref, hd, cwq, s)) * (dk ** -0.5)
        ks[0:lc] = l2n(conv_silu(kc_ref, hd, cwk, lc))
        ks[lc:l_all] = l2n(conv_silu(kl_ref, hd, cwk, s))
        vs[0:lc] = conv_silu(vc_ref, hd, cwv, lc)
        vs[lc:l_all] = conv_silu(vl_ref, hd, cwv, s)

        r = g_ref[0, hd]
        rid8 = lax.broadcasted_iota(jnp.int32, (SUBLANES, l_all), 0)
        aneg = -jnp.exp(alog_ref[hd][:, 0:1])
        dtb = dtb_ref[hd][:, 0:1]
        val = jnp.where(rid8 < 2, _sigmoid(r), aneg * _softplus(r + dtb))
        for i in range(l_all // blk):
            vb = val[:, i * blk:(i + 1) * blk]
            pre = jnp.dot(vb, t_pre, preferred_element_type=F32, precision=lax.Precision.HIGHEST)
            suf = jnp.dot(vb, t_suf, preferred_element_type=F32, precision=lax.Precision.HIGHEST)
            rb = lax.broadcasted_iota(jnp.int32, (SUBLANES, blk), 0)
            gblk = jnp.where(rb == 2, pre, jnp.where(rb == 3, suf, vb))
            grow[:, i * blk:(i + 1) * blk] = gblk
            for p2 in range(blk // LANES):
                g128 = gblk[:, p2 * LANES:(p2 + 1) * LANES]
                rolled = pltpu.roll(g128, c, 1)
                ch = (i * (blk // LANES) + p2) * 2
                grow2[0:1, ch * LANES:(ch + 1) * LANES] = jnp.where(lane1 < c, g128[2:3], rolled[3:4])
                grow2[0:1, (ch + 1) * LANES:(ch + 2) * LANES] = jnp.where(lane1 < c, rolled[2:3], g128[3:4])
        gfull = jnp.concatenate([grow[...], jnp.zeros((LANES - SUBLANES, l_all), F32)], axis=0)
        qcol[...] = gfull.T

        def intra(it, carry, hd=hd):
            r0 = pl.multiple_of(it * (nbc * c), 2 * c)
            q = qs[pl.ds(r0, nbc * c), :].reshape(nbc, c, dk)
            k = ks[pl.ds(r0, nbc * c), :].reshape(nbc, c, dk)
            v = vs[pl.ds(r0, nbc * c), :].reshape(nbc, c, dk)
            col = qcol[pl.ds(r0, nbc * c), :].reshape(nbc, c, LANES)
            g2 = grow2[0:1, pl.ds(pl.multiple_of(it * (nbc * LANES), LANES), nbc * LANES)]
            grw2 = jnp.stack([g2[:, j * LANES:(j + 1) * LANES] for j in range(nbc)], axis=0)
            beta = [col[:, :, d:d + 1] for d in range(N_DIR)]
            gcol = [col[:, :, 2 + d:3 + d] for d in range(N_DIR)]
            beta2 = jnp.where(fwd_half, beta[0], beta[1])
            gcol2 = jnp.where(fwd_half, gcol[0], gcol[1])
            kb16 = k.astype(BF16)
            both = _bmm_nt(jnp.concatenate([q.astype(BF16), kb16], axis=1),
                           jnp.concatenate([kb16, kb16], axis=1))
            decay2 = jnp.exp(jnp.where(keep2, gcol2 - grw2, NEG_BIG))
            lm2 = jnp.where(strict2, beta2 * both[:, c:] * decay2, 0.0)
            qkd2 = (both[:, :c] * decay2).astype(BF16)
            half = nbc // 2
            tinv4 = _inv_unit_triangular_x4(jnp.concatenate([lm2[:half], lm2[half:]], axis=2))
            tinv2 = jnp.concatenate([tinv4[:, :, :2 * c], tinv4[:, :, 2 * c:]], axis=0)
            eg = [jnp.exp(g) for g in gcol]
            glast = [gcol[0][:, c - 1:c, :], gcol[1][:, 0:1, :]]
            rhs = [jnp.concatenate([v * beta[d], (k * beta[d]) * eg[d]], axis=2).astype(BF16) for d in range(N_DIR)]
            zero = jnp.zeros((nbc, c, 2 * dk), BF16)
            rhs_bd = jnp.concatenate([jnp.concatenate([rhs[0], zero], axis=2),
                                      jnp.concatenate([zero, rhs[1]], axis=2)], axis=1)
            uw2 = _bmm(tinv2.astype(BF16), rhs_bd).astype(BF16)
            uw_bd = jnp.concatenate([jnp.concatenate([uw2[:, :, :2 * dk], zero], axis=2),
                                     jnp.concatenate([zero, uw2[:, :, 2 * dk:]], axis=2)], axis=1)
            kd = jnp.concatenate([k * jnp.exp(glast[d] - gcol[d]) for d in range(N_DIR)], axis=1)
            kdt2 = jnp.swapaxes(kd, 1, 2).astype(BF16)
            m2 = _bmm(jnp.concatenate([kdt2, qkd2], axis=1), uw_bd)
            def split_pairs(t):
                t = t.reshape((npb, 2) + t.shape[1:])
                return t[:, 0], t[:, 1]
            for d in range(N_DIR):
                m = m2[:, :, d * 2 * dk:(d + 1) * 2 * dk]
                a_mat = jnp.where(eye_k, jnp.exp(glast[d]), 0.0) - m[:, :dk, dk:]
                q_mat = q * eg[d] - m[:, dk:, dk:]
                parts = [split_pairs(t) for t in (a_mat, m[:, :dk, :dk], q_mat, m[:, dk:, :dk])]
                (a1, b1, q1, o1), (a2, b2, q2, o2) = [[p[first ^ d] for p in parts] for first in (0, 1)]
                mm = _bmm(jnp.concatenate([a2, q2], axis=1).astype(BF16),
                          jnp.concatenate([a1, b1], axis=2).astype(BF16))
                q21 = mm[:, dk:, :dk]
                o21 = mm[:, dk:, dk:] + o2
                q_tok = (q1, q21) if d == 0 else (q21, q1)
                o_tok = (o1, o21) if d == 0 else (o21, o1)
                aq_s[hd, d, pl.ds(it * npb, npb)] = jnp.concatenate((mm[:, :dk, :dk],) + q_tok, axis=1).astype(BF16)
                b_s[hd, d, pl.ds(it * npb, npb)] = (mm[:, :dk, dk:] + b2).astype(BF16)
                op_s[hd, d, pl.ds(it * npb, npb)] = jnp.concatenate(o_tok, axis=1).astype(BF16)
            return carry
        for it in range(nct // nbc):
            intra(it, 0)

    st_s[...] = jnp.zeros_like(st_s)
    o_s[...] = jnp.zeros_like(o_s)
    ncp = ncc // 2
    nlp = ncl // 2

    def recur(t, carry):
        pf = t
        pb = jnp.where(t < ncp, ncp - 1 - t, 2 * ncp + nlp - 1 - t)
        chains = [(hd, d, pi) for hd in range(hg) for d, pi in ((0, pf), (1, pb))]
        loaded = []
        for hd, d, pi in chains:
            ro = pl.multiple_of(jnp.where(t >= ncp, pi * (2 * c) - lc, s), 2 * c)
            loaded.append((aq_s[hd, d, pi], st_s[hd, d], b_s[hd, d, pi], op_s[hd, d, pi],
                           o_s[hd, pl.ds(ro, 2 * c), :], ro))
        results = []
        for aq, st, bm, om, o_old, ro in loaded:
            x = _dot(aq, st.astype(BF16))
            results.append((x[:dk] + bm.astype(F32), o_old + x[dk:] + om.astype(F32), ro))
        for (hd, d, pi), (st_new, o_new, ro) in zip(chains, results):
            st_s[hd, d] = st_new
            o_s[hd, pl.ds(ro, 2 * c), :] = o_new
        return carry
    lax.fori_loop(0, ncp + nlp, recur, 0)

    for hd in range(hg):
        z = z_ref[0, hd].astype(F32)
        o_ref[0, :, hd * dk:(hd + 1) * dk] = (
            (_rms_rows(o_s[hd, 0:s]) * nw_ref[...]).astype(F32) * _silu(z)).astype(o_ref.dtype)


def _gdn(pc, plat, grow, alog8, dtb8, conv_w, norm_w, lc, s):
    b = plat.shape[0]
    heads = GDN_HEADS
    l_all = lc + s
    nct = l_all // CHUNK
    assert lc % CHUNK == 0 and s % CHUNK == 0 and l_all % (2 * LANES) == 0
    assert GDN_CHUNK_BATCH % 2 == 0 and nct % GDN_CHUNK_BATCH == 0
    assert lc % (2 * CHUNK) == 0 and s % (4 * CHUNK) == 0
    hg = GDN_HEADS_PER_STEP
    assert heads % hg == 0
    def slot(base):
        return lambda i, h: (i, base // hg + h, 0, 0)
    def cw(base):
        return lambda i, h: (0, base // hg + h)
    return pl.pallas_call(
        functools.partial(_gdn_kernel, lc=lc, s=s, hg=hg, nbc=GDN_CHUNK_BATCH),
        grid=(b, heads // hg),
        in_specs=[pl.BlockSpec((1, hg, lc, LANES), slot(0)),
                  pl.BlockSpec((1, hg, lc, LANES), slot(heads)),
                  pl.BlockSpec((1, hg, lc, LANES), slot(2 * heads)),
                  pl.BlockSpec((1, hg, s, LANES), slot(heads)),
                  pl.BlockSpec((1, hg, s, LANES), slot(2 * heads)),
                  pl.BlockSpec((1, hg, s, LANES), slot(3 * heads)),
                  pl.BlockSpec((1, hg, s, LANES), slot(4 * heads)),
                  pl.BlockSpec((1, hg, SUBLANES, l_all), lambda i, h: (i, h, 0, 0)),
                  pl.BlockSpec((hg, SUBLANES, LANES), lambda i, h: (h, 0, 0)),
                  pl.BlockSpec((hg, SUBLANES, LANES), lambda i, h: (h, 0, 0)),
                  pl.BlockSpec((4, hg * LANES), cw(0)),
                  pl.BlockSpec((4, hg * LANES), cw(heads)),
                  pl.BlockSpec((4, hg * LANES), cw(2 * heads)),
                  pl.BlockSpec((1, LANES), lambda i, h: (0, 0))],
        out_specs=pl.BlockSpec((1, s, hg * LANES), lambda i, h: (i, 0, h)),
        out_shape=jax.ShapeDtypeStruct((b, s, heads * LANES), BF16),
        scratch_shapes=[pltpu.VMEM((l_all, LANES), F32),
                        pltpu.VMEM((l_all, LANES), F32),
                        pltpu.VMEM((l_all, LANES), F32),
                        pltpu.VMEM((l_all, LANES), F32),
                        pltpu.VMEM((SUBLANES, l_all), F32),
                        pltpu.VMEM((SUBLANES, nct * LANES), F32),
                        pltpu.VMEM((s + 2 * SUBLANES, LANES), F32),
                        pltpu.VMEM((hg, N_DIR, nct // 2, LANES + 2 * CHUNK, LANES), BF16),
                        pltpu.VMEM((hg, N_DIR, nct // 2, LANES, LANES), BF16),
                        pltpu.VMEM((hg, N_DIR, nct // 2, 2 * CHUNK, LANES), BF16),
                        pltpu.VMEM((hg, N_DIR, LANES, LANES), F32),
                        pltpu.VMEM((hg, s + 2 * CHUNK, LANES), F32)],
        compiler_params=_cparams(("parallel", "arbitrary")),
        name="gdn",
    )(pc, pc, pc, plat, plat, plat, plat, grow, alog8, dtb8, conv_w, conv_w, conv_w, norm_w)


def _out_kernel(hs_ref, y_ref, gdn_ref, w_ref, x_ref, gm_ref, scf_ref, shf_ref, nw1_ref, nw2_ref,
                x1_ref, h2_ref, *, heads):
    half = heads * LANES
    a2 = nw2_ref[...] * (1.0 + scf_ref[0])
    tm = x_ref.shape[1]
    for r0 in range(0, tm, OUT_SUB_ROWS):
        rs = slice(r0, r0 + OUT_SUB_ROWS)
        y = jnp.concatenate([y_ref[0, j, rs, :] for j in range(heads)], axis=-1).astype(F32)
        lru = (hs_ref[0, rs, :] * _gelu_tanh(y)).astype(BF16)
        m = _dot(lru, w_ref[0:half, :]) + _dot(gdn_ref[0, rs, :], w_ref[half:, :])
        x1 = x_ref[0, rs, :] + gm_ref[0] * (_rms_rows(m) * nw1_ref[...])
        x1_ref[0, rs, :] = x1
        h2_ref[0, rs, :] = (_rms_rows(x1) * a2 + shf_ref[0]).astype(BF16)


def _out_proj(hs, plat, gdn, w_out, x, gm, scf, shf, nw1, nw2, tm):
    b, s, d = x.shape
    heads = LRU_HEADS
    dm = w_out.shape[0]
    row = lambda i, m: (i, m, 0)
    vec = lambda i, m: (i, 0, 0)
    fix = lambda i, m: (0, 0)
    return pl.pallas_call(
        functools.partial(_out_kernel, heads=heads),
        grid=(b, s // tm),
        in_specs=[pl.BlockSpec((1, tm, heads * LANES), row),
                  pl.BlockSpec((1, heads, tm, LANES), lambda i, m: (i, 0, m, 0)),
                  pl.BlockSpec((1, tm, dm - heads * LANES), row),
                  pl.BlockSpec((dm, d), fix),
                  pl.BlockSpec((1, tm, d), row),
                  pl.BlockSpec((1, 1, d), vec),
                  pl.BlockSpec((1, 1, d), vec),
                  pl.BlockSpec((1, 1, d), vec),
                  pl.BlockSpec((1, d), fix),
                  pl.BlockSpec((1, d), fix)],
        out_specs=[pl.BlockSpec((1, tm, d), row), pl.BlockSpec((1, tm, d), row)],
        out_shape=[jax.ShapeDtypeStruct((b, s, d), F32), jax.ShapeDtypeStruct((b, s, d), BF16)],
        compiler_params=_cparams(("parallel", "parallel")),
        name="out_proj",
    )(hs, plat, gdn, w_out, x, gm, scf, shf, nw1, nw2)


def _mlp_kernel(h_ref, w1_ref, w2_ref, x1_ref, gf_ref, nw_ref, o_ref):
    f = pl.program_id(2)

    @pl.when(f == 0)
    def _():
        o_ref[0] = jnp.zeros_like(o_ref[0])

    hid = jnp.maximum(_dot(h_ref[0], w1_ref[...]), 0.0)
    o_ref[0] = o_ref[0] + _dot((hid * hid).astype(BF16), w2_ref[...])

    @pl.when(f == pl.num_programs(2) - 1)
    def _():
        o_ref[0] = x1_ref[0] + gf_ref[0] * (_rms_rows(o_ref[0]) * nw_ref[...])


def _mlp(h2, w1, w2, x1, gf, nw3, tm, tf):
    b, s, d = x1.shape
    ff = w1.shape[1]
    row = lambda i, m, f: (i, m, 0)
    return pl.pallas_call(
        _mlp_kernel,
        grid=(b, s // tm, ff // tf),
        in_specs=[pl.BlockSpec((1, tm, d), row),
                  pl.BlockSpec((d, tf), lambda i, m, f: (0, f)),
                  pl.BlockSpec((tf, d), lambda i, m, f: (f, 0)),
                  pl.BlockSpec((1, tm, d), row),
                  pl.BlockSpec((1, 1, d), lambda i, m, f: (i, 0, 0)),
                  pl.BlockSpec((1, d), lambda i, m, f: (0, 0))],
        out_specs=pl.BlockSpec((1, tm, d), row),
        out_shape=jax.ShapeDtypeStruct((b, s, d), F32),
        compiler_params=_cparams(("parallel", "parallel", "arbitrary"), VMEM_LIMIT_MLP),
        name="mlp",
    )(h2, w1, w2, x1, gf, nw3)


def kernel(x, c, ctx, c_ctx, w_mod, b_mod, norm_w, w_in, lru_conv_w, lru_conv_b, lru_gate_w, lru_gate_b,
           lru_lambda, gdn_conv_w, gdn_a_log, gdn_dt_bias, gdn_norm_w, w_out, w_ff1, w_ff2):
    b, s, d = x.shape
    lc = ctx.shape[1]
    l_all = lc + s
    rows = s // GRID_W
    d_lru = LRU_HEADS * LANES
    d_gdn = GDN_HEADS * LANES
    assert w_mod.shape[0] == 1 and b == SUBLANES and d == d_lru + d_gdn

    cc = jnp.concatenate([c, c_ctx[None], jnp.zeros((2 * SUBLANES - b - 1, d), F32)], axis=0)
    mod = _modulation(cc, w_mod[0], b_mod[0])
    sh_m, sc_m, g_m, sh_f, sc_f, g_f = [mod[:b, i * d:(i + 1) * d] for i in range(6)]
    csh_m = jnp.broadcast_to(mod[b:b + 1, 0:d], (b, d))
    csc_m = jnp.broadcast_to(mod[b:b + 1, d:2 * d], (b, d))
    nw = norm_w[0]
    nw0, nw1, nw2, nw3 = [nw[i:i + 1] for i in range(4)]
    v3 = lambda t: t.reshape(b, 1, d)

    w_in0 = w_in[0]
    y_end = 2 * d_lru
    qkv_end = y_end + 3 * d_gdn
    z_end = qkv_end + d_gdn
    w_bf = w_in0.astype(BF16)
    n_gb = w_in0.shape[1] - z_end
    w_gb = jnp.pad(w_in0[:, z_end:], ((0, 0), (0, LANES - n_gb))).astype(BF16)

    p_lat, gb_lat = _inproj_raster(x, nw0, v3(sc_m), v3(sh_m), w_bf, w_gb, col0=d_lru, n=z_end - d_lru,
                                   tm=1024, tn=1024)
    p_ctx, gb_ctx = _inproj_raster(ctx, nw0, v3(csc_m), v3(csh_m), w_bf, w_gb, col0=y_end, n=qkv_end - y_end,
                                   tm=lc, tn=1024)
    xl = _inproj_tb(x.reshape(b, rows, GRID_W, d), nw0, v3(sc_m), v3(sh_m), w_bf, n=d_lru, r=rows, nbh=b // 2)
    xl = xl.reshape(LRU_HEADS, GRID_W // SUBLANES, rows, SUBLANES * b, LANES)
    xc = _inproj_tb(ctx.reshape(b, lc // SUBLANES, SUBLANES, d), nw0, v3(csc_m), v3(csh_m), w_bf, n=d_lru,
                    r=2 * SUBLANES, nbh=b)

    gw = lru_gate_w[0]
    wg = (0.5 * jnp.transpose(gw, (2, 3, 0, 1, 4))).reshape(LRU_HEADS, LANES, 4 * LANES).astype(BF16)
    gbias = 0.5 * jnp.transpose(lru_gate_b[0].reshape(N_DIR, 2, LRU_HEADS, LANES), (2, 0, 1, 3))
    gbias = gbias.reshape(LRU_HEADS, 1, 4 * LANES)
    lam2 = jnp.transpose(lru_lambda[0].reshape(N_DIR, LRU_HEADS, LANES), (1, 0, 2)).reshape(1, 2 * d_lru)
    hs = _lru(xc, xl, lru_conv_w[0], lru_conv_b[0].reshape(1, d_lru), wg, gbias, lam2, b, lc, s)

    gb = jnp.concatenate([gb_ctx, gb_lat], axis=1)[:, :, :n_gb]
    gb = gb.reshape(b, l_all, 2, N_DIR, GDN_HEADS)
    grow = jnp.transpose(gb, (0, 4, 2, 3, 1)).reshape(b, GDN_HEADS, 2 * N_DIR, l_all)
    grow = jnp.pad(grow, ((0, 0), (0, 0), (0, SUBLANES - 2 * N_DIR), (0, 0)))
    def rows8(t):
        t = jnp.transpose(t, (1, 0))[:, :, None]
        t = jnp.pad(t, ((0, 0), (2, SUBLANES - 2 - N_DIR), (0, 0)))
        return jnp.broadcast_to(t, (GDN_HEADS, SUBLANES, LANES)).astype(F32)
    gdn = _gdn(p_ctx, p_lat, grow, rows8(gdn_a_log[0]), rows8(gdn_dt_bias[0]), gdn_conv_w[0],
               gdn_norm_w[0].reshape(1, LANES), lc, s)

    x1, h2 = _out_proj(hs, p_lat, gdn, w_out[0].astype(BF16), x, v3(g_m), v3(sc_f), v3(sh_f),
                       nw1, nw2, tm=512)
    return _mlp(h2, w_ff1[0].astype(BF16), w_ff2[0].astype(BF16), x1, v3(g_f), nw3, tm=512, tf=2048)
```
